```python
import math
import jax
import jax.numpy as jnp
from jax import lax
import numpy as np


D_MODEL = 1024
BATCH = 16
SEQ = 2048
DEPTH = 1

CTX_LEN = 256
GRID_W = 64
GDN_HEADS = 8
GDN_DK = 64
GDN_DV = 64
SSD_HEADS = 8
SSD_HEADDIM = 64
SSD_GROUPS = 2
SSD_STATE = 128
CONV_W = 5
CHUNK = 64
N_EXPERTS = 32
TOP_K = 4
D_EXPERT = 1024
SWIGLU_LIMIT = 7.0
SWIGLU_ALPHA = 1.702
MOE_BLOCK = 256
EPS = 1e-6

GDN_QK = GDN_HEADS * GDN_DK
GDN_V = GDN_HEADS * GDN_DV
GDN_CONV_CH = 2 * GDN_QK + GDN_V
GDN_COLS = GDN_CONV_CH + GDN_V + 4 * GDN_HEADS
SSD_INNER = SSD_HEADS * SSD_HEADDIM
SSD_BC = SSD_GROUPS * SSD_STATE
SSD_CONV_CH = SSD_INNER + 2 * SSD_BC
SSD_COLS = SSD_CONV_CH + SSD_INNER + 2 * SSD_HEADS
IN_COLS = GDN_COLS + SSD_COLS
D_MIX = GDN_V + SSD_INNER

kernel_name = "hybrid_gdn_ssd_moe_diffusion_layer"


def rms_norm(x, w):
    xf = x.astype(jnp.float32)
    y = xf * lax.rsqrt(jnp.mean(xf * xf, axis=-1, keepdims=True) + EPS)
    return (y * w.astype(jnp.float32)).astype(x.dtype)


def l2norm(t):
    return t * lax.rsqrt(jnp.sum(t * t, axis=-1, keepdims=True) + EPS)


def modulate(x, w, shift, scale):
    return rms_norm(x, w) * (1 + scale) + shift


def dwconv(x, w):
    ch = x.shape[-1]
    return lax.conv_general_dilated(
        x, w[:, None, :].astype(x.dtype), window_strides=(1,),
        padding=[(CONV_W // 2, CONV_W // 2)],
        dimension_numbers=('NWC', 'WIO', 'NWC'), feature_group_count=ch)


def gdn_scan(k, v, beta, g, s0, q):
    Bn, H, T, dk = k.shape
    dv = v.shape[-1]
    nc = T // CHUNK
    ck = lambda t: t.reshape(t.shape[:2] + (nc, CHUNK) + t.shape[3:])
    k, v, beta, g = ck(k), ck(v), ck(beta), ck(g)
    G = jnp.cumsum(g, axis=-1)
    idx = jnp.arange(CHUNK)
    rel = G[..., :, None] - G[..., None, :]
    kb = k * beta[..., None]
    strict = idx[:, None] > idx[None, :]
    lmat = jnp.einsum('bhncd,bhnsd->bhncs', kb, k) * jnp.exp(jnp.where(strict, rel, -jnp.inf))
    mmat = lmat + jnp.eye(CHUNK, dtype=lmat.dtype)
    rhs = jnp.concatenate([v * beta[..., None], kb * jnp.exp(G)[..., None]], axis=-1)
    sol = lax.linalg.triangular_solve(mmat, rhs, left_side=True, lower=True, unit_diagonal=True)
    U, W = sol[..., :dv], sol[..., dv:]
    g_last = G[..., -1]
    k_dec = k * jnp.exp(g_last[..., None] - G)[..., None]
    if s0 is None:
        s0 = jnp.zeros((Bn, H, dk, dv), jnp.float32)
    collect = q is not None

    def step(S, inp):
        w_c, u_c, kd_c, gl_c = inp
        v_new = u_c - jnp.einsum('bhcd,bhde->bhce', w_c, S)
        s_next = S * jnp.exp(gl_c)[..., None, None] + jnp.einsum('bhcd,bhce->bhde', kd_c, v_new)
        return s_next, ((S, v_new) if collect else None)

    mv = lambda t: jnp.moveaxis(t, 2, 0)
    s_fin, ys = lax.scan(step, s0, (mv(W), mv(U), mv(k_dec), mv(g_last)))
    if not collect:
        return None, s_fin
    s_starts = jnp.moveaxis(ys[0], 0, 2)
    v_new = jnp.moveaxis(ys[1], 0, 2)
    q = ck(q)
    incl = idx[:, None] >= idx[None, :]
    attn = jnp.einsum('bhncd,bhnsd->bhncs', q, k) * jnp.exp(jnp.where(incl, rel, -jnp.inf))
    o = (jnp.einsum('bhncd,bhnde->bhnce', q * jnp.exp(G)[..., None], s_starts)
         + jnp.einsum('bhncs,bhnse->bhnce', attn, v_new))
    return o.reshape(Bn, H, T, dv), s_fin


def gdn_group(u, conv_w, A_log, dt_bias, norm_w, init, with_out):
    Bn, T, _ = u.shape
    H = GDN_HEADS
    qkv = jax.nn.silu(dwconv(u[..., :GDN_CONV_CH], conv_w)).astype(jnp.float32)
    heads = lambda t, d: t.reshape(Bn, T, H, d).transpose(0, 2, 1, 3)
    k = l2norm(heads(qkv[..., GDN_QK:2 * GDN_QK], GDN_DK))
    v = heads(qkv[..., 2 * GDN_QK:], GDN_DV)
    q = (l2norm(heads(qkv[..., :GDN_QK], GDN_DK)) * GDN_DK ** -0.5) if with_out else None
    ba = u[..., GDN_CONV_CH + GDN_V:].astype(jnp.float32).reshape(Bn, T, 4, H).transpose(2, 0, 3, 1)
    beta = jax.nn.sigmoid(ba[:2])
    g = -jnp.exp(A_log)[:, None, :, None] * jax.nn.softplus(ba[2:] + dt_bias[:, None, :, None])
    s0f, s0b = (None, None) if init is None else init
    flip = lambda t: None if t is None else jnp.flip(t, axis=2)
    o_f, s_f = gdn_scan(k, v, beta[0], g[0], s0f, q)
    o_b, s_b = gdn_scan(flip(k), flip(v), flip(beta[1]), flip(g[1]), s0b, flip(q))
    if not with_out:
        return None, (s_f, s_b)
    z = heads(u[..., GDN_CONV_CH:GDN_CONV_CH + GDN_V].astype(jnp.float32), GDN_DV)
    o = rms_norm(o_f + flip(o_b), norm_w) * jax.nn.silu(z)
    return o.transpose(0, 2, 1, 3).reshape(Bn, T, GDN_V).astype(u.dtype), (s_f, s_b)


def ssd_scan(x, bm, dt, A, s0, cm):
    Bn, G, Hg, T, P = x.shape
    N = bm.shape[-1]
    nc = T // CHUNK
    x = x.reshape(Bn, G, Hg, nc, CHUNK, P)
    bm = bm.reshape(Bn, G, nc, CHUNK, N)
    dt = dt.reshape(Bn, G, Hg, nc, CHUNK)
    acum = jnp.cumsum(dt * A[:, :, None, None], axis=-1)
    a_last = acum[..., -1]
    s_chunk = jnp.einsum('bgnsk,bghns,bghnsp->bghnkp', bm, jnp.exp(a_last[..., None] - acum) * dt, x)
    if s0 is None:
        s0 = jnp.zeros((Bn, G, Hg, N, P), jnp.float32)
    collect = cm is not None

    def step(s, inp):
        sc, al = inp
        return s * jnp.exp(al)[..., None, None] + sc, (s if collect else None)

    s_fin, s_starts = lax.scan(step, s0, (jnp.moveaxis(s_chunk, 3, 0), jnp.moveaxis(a_last, 3, 0)))
    if not collect:
        return None, s_fin
    s_starts = jnp.moveaxis(s_starts, 0, 3)
    cm = cm.reshape(Bn, G, nc, CHUNK, N)
    idx = jnp.arange(CHUNK)
    incl = idx[:, None] >= idx[None, :]
    seg = jnp.exp(jnp.where(incl, acum[..., :, None] - acum[..., None, :], -jnp.inf)) * dt[..., None, :]
    cb = jnp.einsum('bgnck,bgnsk->bgncs', cm, bm)
    y = (jnp.einsum('bgncs,bghncs,bghnsp->bghncp', cb, seg, x)
         + jnp.einsum('bgnck,bghnkp->bghncp', cm, s_starts) * jnp.exp(acum)[..., None])
    return y.reshape(Bn, G, Hg, T, P), s_fin


def ssd_group(u, conv_w, conv_b, A_log, dt_bias, D_skip, norm_w, init, with_out):
    Bn, T, _ = u.shape
    G, Hg, P, N = SSD_GROUPS, SSD_HEADS // SSD_GROUPS, SSD_HEADDIM, SSD_STATE
    xbc = jax.nn.silu(dwconv(u[..., :SSD_CONV_CH], conv_w) + conv_b).astype(jnp.float32)
    xh = xbc[..., :SSD_INNER].reshape(Bn, T, G, Hg, P).transpose(0, 2, 3, 1, 4)
    bm = xbc[..., SSD_INNER:SSD_INNER + SSD_BC].reshape(Bn, T, G, N).transpose(0, 2, 1, 3)
    cm = xbc[..., SSD_INNER + SSD_BC:].reshape(Bn, T, G, N).transpose(0, 2, 1, 3) if with_out else None
    dt_raw = u[..., SSD_CONV_CH + SSD_INNER:].astype(jnp.float32).reshape(Bn, T, 2, G, Hg).transpose(2, 0, 3, 4, 1)
    dt = jax.nn.softplus(dt_raw + dt_bias.reshape(2, 1, G, Hg, 1))
    A = -jnp.exp(A_log).reshape(2, G, Hg)
    s0f, s0b = (None, None) if init is None else init
    fx = lambda t: jnp.flip(t, axis=3)
    fb = lambda t: None if t is None else jnp.flip(t, axis=2)
    y_f, s_f = ssd_scan(xh, bm, dt[0], A[0], s0f, cm)
    y_b, s_b = ssd_scan(fx(xh), fb(bm), fx(dt[1]), A[1], s0b, fb(cm))
    if not with_out:
        return None, (s_f, s_b)
    y = y_f + fx(y_b) + D_skip.reshape(G, Hg, 1, 1) * xh
    y = y.transpose(0, 3, 1, 2, 4).reshape(Bn, T, G, Hg * P)
    z = jax.nn.silu(u[..., SSD_CONV_CH:SSD_CONV_CH + SSD_INNER].astype(jnp.float32)).reshape(Bn, T, G, Hg * P)
    y = rms_norm(y * z, norm_w.reshape(G, Hg * P))
    return y.reshape(Bn, T, SSD_INNER).astype(u.dtype), (s_f, s_b)


def moe(h, w_router, b_router, w_gate, b_gate, w_up, b_up, w_down, b_down):
    lead = h.shape[:-1]
    t = h.reshape(-1, D_MODEL)
    n_tok = t.shape[0]
    logits = t.astype(jnp.float32) @ w_router.astype(jnp.float32) + b_router.astype(jnp.float32)
    top_v, top_e = lax.top_k(logits, TOP_K)
    gates = jax.nn.softmax(top_v, axis=-1)
    n_assign = n_tok * TOP_K
    e_flat = top_e.reshape(-1)
    tok_flat = jnp.arange(n_assign, dtype=jnp.int32) // TOP_K
    order = jnp.argsort(e_flat)
    e_sorted = e_flat[order]
    counts = jnp.bincount(e_flat, length=N_EXPERTS)
    padded = (counts + MOE_BLOCK - 1) // MOE_BLOCK * MOE_BLOCK
    pad_end = jnp.cumsum(padded)
    pad_start = pad_end - padded
    grp_start = jnp.cumsum(counts) - counts
    dest = pad_start[e_sorted] + jnp.arange(n_assign) - grp_start[e_sorted]
    n_blocks = -(-n_assign // MOE_BLOCK) + N_EXPERTS
    cap = n_blocks * MOE_BLOCK
    row_tok = jnp.zeros((cap,), jnp.int32).at[dest].set(tok_flat[order])
    row_gate = jnp.zeros((cap,), jnp.float32).at[dest].set(gates.reshape(-1)[order])
    block_e = jnp.minimum(jnp.searchsorted(pad_end, jnp.arange(n_blocks) * MOE_BLOCK, side='right'), N_EXPERTS - 1)

    def block_step(out, inp):
        toks, gw, e = inp
        xb = t[toks]
        a = jnp.minimum(xb @ w_gate[e] + b_gate[e], SWIGLU_LIMIT)
        b = jnp.clip(xb @ w_up[e] + b_up[e], -SWIGLU_LIMIT, SWIGLU_LIMIT)
        hid = a * jax.nn.sigmoid(SWIGLU_ALPHA * a) * (b + 1)
        y = hid @ w_down[e] + b_down[e]
        return out.at[toks].add((y * gw[:, None]).astype(out.dtype)), None

    out, _ = lax.scan(block_step, jnp.zeros_like(t),
                      (row_tok.reshape(n_blocks, MOE_BLOCK), row_gate.reshape(n_blocks, MOE_BLOCK), block_e))
    return out.reshape(lead + (D_MODEL,))


def setup_inputs(seed: int = 0) -> dict:
    key = jax.random.key(seed)
    ks = jax.random.split(key, 32)
    f32 = jnp.float32
    L, D, H = DEPTH, D_MODEL, GDN_HEADS
    nrm = lambda k, shape, s: jax.random.normal(k, shape, f32) * s
    gain = lambda k, shape: 1.0 + 0.05 * jax.random.normal(k, shape, f32)

    def dt_bias(k, shape):
        dt = jnp.exp(jax.random.uniform(k, shape, f32, math.log(1e-3), math.log(1e-1)))
        return dt + jnp.log(-jnp.expm1(-dt))

    return {
        "x": nrm(ks[0], (BATCH, SEQ, D), 1.0),
        "c": nrm(ks[1], (BATCH, D), 1.0),
        "ctx": nrm(ks[2], (BATCH, CTX_LEN, D), 1.0),
        "c_ctx": nrm(ks[3], (D,), 1.0),
        "w_ada": nrm(ks[4], (L, D, 6 * D), 0.5 * D ** -0.5),
        "b_ada": nrm(ks[5], (L, 6 * D), 0.02),
        "norm_pre_mix": gain(ks[6], (L, D)),
        "norm_post_mix": gain(ks[7], (L, D)),
        "norm_pre_ffn": gain(ks[8], (L, D)),
        "norm_post_ffn": gain(ks[9], (L, D)),
        "w_in": nrm(ks[10], (L, D, IN_COLS), D ** -0.5),
        "gdn_conv_w": nrm(ks[11], (L, CONV_W, GDN_CONV_CH), CONV_W ** -0.5),
        "gdn_A_log": jnp.log(jax.random.uniform(ks[12], (L, 2, H), f32, 1.0, 16.0)),
        "gdn_dt_bias": dt_bias(ks[13], (L, 2, H)),
        "gdn_norm_w": gain(ks[14], (L, GDN_DV)),
        "ssd_conv_w": nrm(ks[15], (L, CONV_W, SSD_CONV_CH), CONV_W ** -0.5),
        "ssd_conv_b": nrm(ks[16], (L, SSD_CONV_CH), 0.02),
        "ssd_A_log": jnp.log(jax.random.uniform(ks[17], (L, 2, SSD_HEADS), f32, 1.0, 16.0)),
        "ssd_dt_bias": dt_bias(ks[18], (L, 2, SSD_HEADS)),
        "ssd_D": gain(ks[19], (L, SSD_HEADS)),
        "ssd_norm_w": gain(ks[20], (L, SSD_INNER)),
        "w_out": nrm(ks[21], (L, D_MIX, D), D_MIX ** -0.5),
        "w_router": nrm(ks[22], (L, D, N_EXPERTS), D ** -0.5),
        "b_router": nrm(ks[23], (L, N_EXPERTS), 0.01),
        "w_gate": nrm(ks[24], (L, N_EXPERTS, D, D_EXPERT), D ** -0.5),
        "b_gate": nrm(ks[25], (L, N_EXPERTS, D_EXPERT), 0.02),
        "w_up": nrm(ks[26], (L, N_EXPERTS, D, D_EXPERT), D ** -0.5),
        "b_up": nrm(ks[27], (L, N_EXPERTS, D_EXPERT), 0.02),
        "w_down": nrm(ks[28], (L, N_EXPERTS, D_EXPERT, D), D_EXPERT ** -0.5),
        "b_down": nrm(ks[29], (L, N_EXPERTS, D), 0.02),
    }


def reference(x, c, ctx, c_ctx, w_ada, b_ada, norm_pre_mix, norm_post_mix, norm_pre_ffn, norm_post_ffn,
              w_in, gdn_conv_w, gdn_A_log, gdn_dt_bias, gdn_norm_w, ssd_conv_w, ssd_conv_b, ssd_A_log,
              ssd_dt_bias, ssd_D, ssd_norm_w, w_out, w_router, b_router, w_gate, b_gate, w_up, b_up,
              w_down, b_down):
    Bn, n_lat, _ = x.shape
    rows = n_lat // GRID_W
    to_col = lambda t: t.reshape(Bn, rows, GRID_W, -1).transpose(0, 2, 1, 3).reshape(Bn, n_lat, -1)
    from_col = lambda t: t.reshape(Bn, GRID_W, rows, -1).transpose(0, 2, 1, 3).reshape(Bn, n_lat, -1)
    xc = ctx
    for l in range(DEPTH):
        last = l == DEPTH - 1
        mod = (jax.nn.silu(c) @ w_ada[l] + b_ada[l])[:, None, :]
        mod_c = jax.nn.silu(c_ctx) @ w_ada[l] + b_ada[l]
        sh1, sc1, g1, sh2, sc2, g2 = jnp.split(mod, 6, axis=-1)
        csh1, csc1, cg1, csh2, csc2, cg2 = jnp.split(mod_c, 6, axis=-1)
        gdn_p = (gdn_conv_w[l], gdn_A_log[l], gdn_dt_bias[l], gdn_norm_w[l])
        ssd_p = (ssd_conv_w[l], ssd_conv_b[l], ssd_A_log[l], ssd_dt_bias[l], ssd_D[l], ssd_norm_w[l])

        uc = modulate(xc, norm_pre_mix[l], csh1, csc1) @ w_in[l]
        gdn_c, gdn_state = gdn_group(uc[..., :GDN_COLS], *gdn_p, None, not last)
        ssd_c, ssd_state = ssd_group(uc[..., GDN_COLS:], *ssd_p, None, not last)

        u = modulate(x, norm_pre_mix[l], sh1, sc1) @ w_in[l]
        gdn_o, _ = gdn_group(u[..., :GDN_COLS], *gdn_p, gdn_state, True)
        ssd_o, _ = ssd_group(to_col(u[..., GDN_COLS:]), *ssd_p, ssd_state, True)
        m = jnp.concatenate([gdn_o, from_col(ssd_o)], axis=-1) @ w_out[l]
        x = x + g1 * rms_norm(m, norm_post_mix[l])
        f = moe(modulate(x, norm_pre_ffn[l], sh2, sc2), w_router[l], b_router[l], w_gate[l], b_gate[l],
                w_up[l], b_up[l], w_down[l], b_down[l])
        x = x + g2 * rms_norm(f, norm_post_ffn[l])

        if not last:
            mc = jnp.concatenate([gdn_c, ssd_c], axis=-1) @ w_out[l]
            xc = xc + cg1 * rms_norm(mc, norm_post_mix[l])
            fc = moe(modulate(xc, norm_pre_ffn[l], csh2, csc2), w_router[l], b_router[l], w_gate[l], b_gate[l],
                     w_up[l], b_up[l], w_down[l], b_down[l])
            xc = xc + cg2 * rms_norm(fc, norm_post_ffn[l])
    return x
```

```python
import functools

import jax
import jax.numpy as jnp
from jax import lax
from jax.experimental import pallas as pl
from jax.experimental.pallas import tpu as pltpu

F32 = jnp.float32
BF16 = jnp.bfloat16
I32 = jnp.int32

D_MODEL = 1024
GRID_W = 64
GDN_HEADS = 8
GDN_DK = 64
GDN_DV = 64
SSD_HEADS = 8
SSD_HEADDIM = 64
SSD_GROUPS = 2
SSD_STATE = 128
CONV_W = 5
CHUNK = 64
N_EXPERTS = 32
TOP_K = 4
D_EXPERT = 1024
SWIGLU_LIMIT = 7.0
SWIGLU_ALPHA = 1.702
MOE_BLOCK = 256
EPS = 1e-6

GDN_QK = GDN_HEADS * GDN_DK
GDN_V = GDN_HEADS * GDN_DV
GDN_CONV_CH = 2 * GDN_QK + GDN_V
GDN_COLS = GDN_CONV_CH + GDN_V + 4 * GDN_HEADS
SSD_INNER = SSD_HEADS * SSD_HEADDIM
SSD_BC = SSD_GROUPS * SSD_STATE
SSD_CONV_CH = SSD_INNER + 2 * SSD_BC
SSD_COLS = SSD_CONV_CH + SSD_INNER + 2 * SSD_HEADS

LANE = 128
SUBLANE = 8
GDN_COLS_PAD = 17 * LANE
SSD_COLS_PAD = 13 * LANE
VMEM_LIMIT = 56 * 1024 * 1024
NEG = -1e30
ROW_BLOCK = 256


def _dot(a, b):
    return jnp.dot(a, b, preferred_element_type=F32)


def _dot_nt(a, b):
    return lax.dot_general(a, b, (((1,), (1,)), ((), ())), preferred_element_type=F32)


def _dot_f32(a, b):
    return jnp.dot(a, b, preferred_element_type=F32, precision=lax.Precision.HIGHEST)


def _silu(x):
    return x * (1.0 / (1.0 + jnp.exp(-x)))


def _sigmoid(x):
    return 1.0 / (1.0 + jnp.exp(-x))


def _softplus(x):
    return jnp.maximum(x, 0.0) + jnp.log(1.0 + jnp.exp(-jnp.abs(x)))


def _iota(shape, axis):
    return lax.broadcasted_iota(I32, shape, axis)


def _split_dot(x, ones_b16):
    hi = x.astype(BF16)
    lo = (x - hi.astype(F32)).astype(BF16)
    return _dot(hi, ones_b16) + _dot(lo, ones_b16)


def _ada_kernel(c_ref, w_ref, b_ref, o_ref):
    o_ref[...] = _dot_f32(_silu(c_ref[...]), w_ref[...]) + b_ref[...]


def _ada(cc, w_ada, b_ada):
    rows, d = cc.shape
    n = w_ada.shape[1]
    tn = 1024
    return pl.pallas_call(
        _ada_kernel,
        grid=(n // tn,),
        in_specs=[pl.BlockSpec((rows, d), lambda j: (0, 0)),
                  pl.BlockSpec((d, tn), lambda j: (0, j)),
                  pl.BlockSpec((1, tn), lambda j: (0, j))],
        out_specs=pl.BlockSpec((rows, tn), lambda j: (0, j)),
        out_shape=jax.ShapeDtypeStruct((rows, n), F32),
        compiler_params=pltpu.CompilerParams(dimension_semantics=("arbitrary",),
                                             vmem_limit_bytes=VMEM_LIMIT),
        name="ada",
    )(cc, w_ada, b_ada.reshape(1, n))


def _modulated_norm(x, nw, sh, sc):
    ms = jnp.mean(x * x, axis=-1, keepdims=True)
    return (x * lax.rsqrt(ms + EPS) * nw) * (1.0 + sc) + sh


def _inproj_kernel(x_ref, sh_ref, sc_ref, nw_ref, w_ref, o_ref, *, pieces):
    if pieces:
        xt = x_ref[...]
        x = jnp.concatenate([xt[:, i * D_MODEL:(i + 1) * D_MODEL] for i in range(pieces)], axis=0)
    else:
        x = x_ref[...]
    h = _modulated_norm(x, nw_ref[...], sh_ref[...], sc_ref[...])
    o_ref[...] = _dot(h.astype(BF16), w_ref[...])


def _inproj(x, mod3, nw, w_b16, *, mod_row, col_major):
    bn, t, d = x.shape
    cols = w_b16.shape[1]
    if col_major:
        rows = t // GRID_W
        pieces = 8
        tm = pieces * rows
        xin = x.reshape(bn, rows, GRID_W * d)
        x_spec = pl.BlockSpec((None, rows, pieces * d), lambda b, i: (b, 0, i))
    else:
        pieces = 0
        tm = min(512, t)
        xin = x
        x_spec = pl.BlockSpec((None, tm, d), lambda b, i: (b, i, 0))
    if mod_row is None:
        row = lambda b: b
    else:
        row = lambda b: mod_row
    return pl.pallas_call(
        functools.partial(_inproj_kernel, pieces=pieces),
        grid=(bn, t // tm),
        in_specs=[x_spec,
                  pl.BlockSpec((None, 1, d), lambda b, i: (row(b), 0, 0)),
                  pl.BlockSpec((None, 1, d), lambda b, i: (row(b), 0, 1)),
                  pl.BlockSpec((1, d), lambda b, i: (0, 0)),
                  pl.BlockSpec((d, cols), lambda b, i: (0, 0))],
        out_specs=pl.BlockSpec((None, tm, cols), lambda b, i: (b, i, 0)),
        out_shape=jax.ShapeDtypeStruct((bn, t, cols), F32),
        compiler_params=pltpu.CompilerParams(dimension_semantics=("arbitrary", "arbitrary"),
                                             vmem_limit_bytes=VMEM_LIMIT),
        name="inproj_cm" if col_major else "inproj",
    )(xin, mod3, mod3, nw, w_b16)


def _conv_silu(src_ref, t, pad_ref, w_ref, bias, dst_ref, post=None):
    width = src_ref.shape[-1]
    zero = jnp.zeros((SUBLANE, width), F32)
    pad_ref[0:SUBLANE, :] = zero
    pad_ref[pl.ds(SUBLANE + t, SUBLANE), :] = zero
    rb = ROW_BLOCK * LANE // width
    nb = t // rb

    def copy(i, c):
        r0 = pl.multiple_of(i * rb, rb)
        pad_ref[pl.ds(SUBLANE + r0, rb), :] = src_ref[pl.ds(r0, rb), :]
        return c

    lax.fori_loop(0, nb, copy, 0)
    w = w_ref[...]

    def body(i, c):
        r0 = pl.multiple_of(i * rb, rb)
        win = pad_ref[pl.ds(r0, rb + 2 * SUBLANE), :]
        acc = None
        for j in range(CONV_W):
            off = SUBLANE - CONV_W // 2 + j
            tap = pltpu.roll(win, rb + 2 * SUBLANE - off, 0)[0:rb, :] * w[j:j + 1, :]
            acc = tap if acc is None else acc + tap
        if bias is not None:
            acc = acc + bias
        y = _silu(acc)
        if post is not None:
            y = post(y)
        dst_ref[pl.ds(r0, rb), :] = y
        return c

    lax.fori_loop(0, nb, body, 0)


def _chunk_cumsums(x, rows_in_chunk):
    n = x.shape[0]
    pre = x
    suf = x
    s = 1
    while s < CHUNK:
        pre = pre + jnp.where(rows_in_chunk >= s, pltpu.roll(pre, s, 0), 0.0)
        suf = suf + jnp.where(rows_in_chunk < CHUNK - s, pltpu.roll(suf, n - s, 0), 0.0)
        s *= 2
    return pre, suf


def _stack_masked(x, groups):
    r, n = x.shape
    w = n // groups
    lane_group = _iota((r, n), 1) // w
    return jnp.concatenate([jnp.where(lane_group == g, x, 0.0) for g in range(groups)], axis=0)


def _gdn_kernel(q_ref, k_ref, v_ref, z_ref, g_ref, kc_ref, vc_ref, gc_ref,
                wq_ref, wk_ref, wv_ref, gpar_ref, nw_ref, o_ref,
                pad_ref, qn_ref, kn_ref, vv_ref, gb_ref, u_ref, w_ref, at_ref, qe_ref,
                kdt_ref, egl_ref, oacc_ref, s_ref, *, t_lat, t_ctx):
    hp = pl.program_id(1)
    ones_bd = (_iota((LANE, LANE), 0) // GDN_DK == _iota((LANE, LANE), 1) // GDN_DK).astype(BF16)
    lane64 = _iota((CHUNK, LANE), 1)
    first = lane64 < GDN_DK
    ii = _iota((CHUNK, LANE), 0)
    jj = lane64 % CHUNK
    eye = (ii == jj).astype(F32)
    bd_mask = (_iota((LANE, LANE), 0) // GDN_DK) == (_iota((LANE, LANE), 1) // GDN_DV)
    gpar = gpar_ref[...]
    a_row = jnp.exp(gpar[0:1, :])
    dtb_row = gpar[1:2, :]

    def l2norm(scale):
        def f(y):
            ss = _split_dot(y * y, ones_bd)
            return y * lax.rsqrt(ss + EPS) * scale
        return f

    def gate_block(src_ref, t):
        lane = _iota((ROW_BLOCK, LANE), 1)
        ric = _iota((ROW_BLOCK, LANE), 0) % CHUNK
        shift = (LANE - 2 * hp) % LANE

        def body(i, c):
            r0 = pl.multiple_of(i * ROW_BLOCK, ROW_BLOCK)
            x = src_ref[pl.ds(r0, ROW_BLOCK), :]
            act = jnp.where(lane < 2 * GDN_HEADS, _sigmoid(x), -a_row * _softplus(x + dtb_row))
            pre, suf = _chunk_cumsums(act, ric)
            blk = jnp.where(lane < 2 * GDN_HEADS, act, jnp.where(lane < 3 * GDN_HEADS, pre, suf))
            gb_ref[pl.ds(r0, ROW_BLOCK), :] = pltpu.roll(blk, shift, 1)
            return c

        lax.fori_loop(0, t // ROW_BLOCK, body, 0)

    def expand(tile, c0):
        return jnp.where(first, tile[:, c0:c0 + 1], tile[:, c0 + 1:c0 + 2])

    def intra(n, d, with_out):
        r0 = pl.multiple_of(n * CHUNK, CHUNK)
        k = kn_ref[pl.ds(r0, CHUNK), :]
        v = vv_ref[pl.ds(r0, CHUNK), :]
        g = gb_ref[pl.ds(r0, CHUNK), :]
        gt = jnp.concatenate([g, g], axis=0).T
        beta = expand(g, GDN_HEADS * d)
        gx = expand(g, 2 * GDN_HEADS + GDN_HEADS * d)
        c0 = 2 * GDN_HEADS + GDN_HEADS * d
        grow = jnp.where(first[0:1, :], gt[c0:c0 + 1, :], gt[c0 + 1:c0 + 2, :])
        rel = gx - grow
        if d == 0:
            strict, incl, gl = ii > jj, ii >= jj, gx[CHUNK - 1:CHUNK, :]
        else:
            strict, incl, gl = ii < jj, ii <= jj, gx[0:1, :]
        kb = k * beta
        k_bd = _stack_masked(k, 2).astype(BF16)
        a = _dot_nt(kb.astype(BF16), k_bd) * jnp.exp(jnp.where(strict, rel, NEG))
        nm = -a
        p = eye + nm
        for _ in range(5):
            nm = _dot(nm.astype(BF16), _stack_masked(nm, 2).astype(BF16))
            p = p + _dot(p.astype(BF16), _stack_masked(nm, 2).astype(BF16))
        eg = jnp.exp(gx)
        rhs = jnp.concatenate([v * beta, kb * eg], axis=1)
        rhs_bd = jnp.concatenate([jnp.where(jnp.concatenate([first, first], axis=1), rhs, 0.0),
                                  jnp.where(jnp.concatenate([first, first], axis=1), 0.0, rhs)], axis=0)
        uw = _dot(p.astype(BF16), rhs_bd.astype(BF16))
        u_ref[d, pl.ds(r0, CHUNK), :] = uw[:, :LANE]
        w_ref[d, pl.ds(r0, CHUNK), :] = uw[:, LANE:].astype(BF16)
        kdec = k * jnp.exp(gl - gx)
        kdt = jnp.concatenate([kdec, jnp.zeros_like(kdec)], axis=0).T
        kdt_ref[d, pl.ds(pl.multiple_of(n * LANE, LANE), LANE), :] = kdt.astype(BF16)
        egl_ref[d, n] = jnp.broadcast_to(jnp.exp(gl), (SUBLANE, LANE))
        if with_out:
            q = qn_ref[pl.ds(r0, CHUNK), :]
            at = _dot_nt(q.astype(BF16), k_bd) * jnp.exp(jnp.where(incl, rel, NEG))
            at_ref[d, pl.ds(r0, CHUNK), :] = at.astype(BF16)
            qe_ref[d, pl.ds(r0, CHUNK), :] = (q * eg).astype(BF16)

    def scan_step(n, d, with_out):
        r0 = pl.multiple_of(n * CHUNK, CHUNK)
        s = s_ref[d]
        sb = s.astype(BF16)
        vnew = u_ref[d, pl.ds(r0, CHUNK), :] - _dot(w_ref[d, pl.ds(r0, CHUNK), :], sb)
        vb = vnew.astype(BF16)
        if with_out:
            o = _dot(qe_ref[d, pl.ds(r0, CHUNK), :], sb) + _dot(at_ref[d, pl.ds(r0, CHUNK), :],
                                                               _stack_masked(vnew, 2).astype(BF16))
            oacc_ref[pl.ds(r0, CHUNK), :] += o
        kdt = kdt_ref[d, pl.ds(pl.multiple_of(n * LANE, LANE), LANE), :]
        upd = _dot(kdt, jnp.concatenate([vb, jnp.zeros_like(vb)], axis=0))
        s_ref[d] = s * egl_ref[d, n][0:1, :] + jnp.where(bd_mask, upd, 0.0)

    def phase(t, k_src, v_src, g_src, q_src, with_out):
        nc = t // CHUNK
        _conv_silu(k_src, t, pad_ref, wk_ref, None, kn_ref, l2norm(1.0))
        _conv_silu(v_src, t, pad_ref, wv_ref, None, vv_ref)
        if with_out:
            _conv_silu(q_src, t, pad_ref, wq_ref, None, qn_ref, l2norm(GDN_DK ** -0.5))
        gate_block(g_src, t)

        def intra_body(i, c):
            intra(i, 0, with_out)
            intra(i, 1, with_out)
            return c

        lax.fori_loop(0, nc, intra_body, 0)

        def scan_body(i, c):
            scan_step(i, 0, with_out)
            scan_step(nc - 1 - i, 1, with_out)
            return c

        lax.fori_loop(0, nc, scan_body, 0)

    s_ref[...] = jnp.zeros(s_ref.shape, F32)
    oacc_ref[...] = jnp.zeros(oacc_ref.shape, F32)
    phase(t_ctx, kc_ref, vc_ref, gc_ref, None, False)
    phase(t_lat, k_ref, v_ref, g_ref, q_ref, True)

    nw = nw_ref[...]

    def out_body(i, c):
        r0 = pl.multiple_of(i * ROW_BLOCK, ROW_BLOCK)
        o = oacc_ref[pl.ds(r0, ROW_BLOCK), :]
        ms = _split_dot(o * o, ones_bd) * (1.0 / GDN_DV)
        y = o * lax.rsqrt(ms + EPS) * nw * _silu(z_ref[pl.ds(r0, ROW_BLOCK), :])
        o_ref[pl.ds(r0, ROW_BLOCK), :] = y.astype(o_ref.dtype)
        return c

    lax.fori_loop(0, t_lat // ROW_BLOCK, out_body, 0)


def _gdn(u_lat, u_ctx, conv_w, gpar, nw2):
    bn, t, _ = u_lat.shape
    tc = u_ctx.shape[1]
    npairs = GDN_HEADS // 2
    nc = t // CHUNK
    col = lambda off: (lambda b, h: (b, 0, off + h))
    lat = lambda off: pl.BlockSpec((None, t, LANE), col(off))
    ctx = lambda off: pl.BlockSpec((None, tc, LANE), col(off))
    gate_blk = GDN_CONV_CH // LANE + GDN_V // LANE
    fixed = lambda b, h: (b, 0, gate_blk)
    cw = lambda off: pl.BlockSpec((CONV_W, LANE), lambda b, h: (0, off + h))
    return pl.pallas_call(
        functools.partial(_gdn_kernel, t_lat=t, t_ctx=tc),
        grid=(bn, npairs),
        in_specs=[lat(0), lat(npairs), lat(2 * npairs), lat(3 * npairs),
                  pl.BlockSpec((None, t, LANE), fixed),
                  ctx(npairs), ctx(2 * npairs), pl.BlockSpec((None, tc, LANE), fixed),
                  cw(0), cw(npairs), cw(2 * npairs),
                  pl.BlockSpec((SUBLANE, LANE), lambda b, h: (0, 0)),
                  pl.BlockSpec((1, LANE), lambda b, h: (0, 0))],
        out_specs=pl.BlockSpec((None, t, LANE), lambda b, h: (b, 0, h)),
        out_shape=jax.ShapeDtypeStruct((bn, t, GDN_V), BF16),
        scratch_shapes=[pltpu.VMEM((t + 2 * SUBLANE, LANE), F32),
                        pltpu.VMEM((t, LANE), F32),
                        pltpu.VMEM((t, LANE), F32),
                        pltpu.VMEM((t, LANE), F32),
                        pltpu.VMEM((t, LANE), F32),
                        pltpu.VMEM((2, t, LANE), F32),
                        pltpu.VMEM((2, t, LANE), BF16),
                        pltpu.VMEM((2, t, LANE), BF16),
                        pltpu.VMEM((2, t, LANE), BF16),
                        pltpu.VMEM((2, nc * LANE, LANE), BF16),
                        pltpu.VMEM((2, nc, SUBLANE, LANE), F32),
                        pltpu.VMEM((t, LANE), F32),
                        pltpu.VMEM((2, LANE, LANE), F32)],
        compiler_params=pltpu.CompilerParams(dimension_semantics=("arbitrary", "arbitrary"),
                                             vmem_limit_bytes=VMEM_LIMIT),
        name="gdn",
    )(u_lat, u_lat, u_lat, u_lat, u_lat, u_ctx, u_ctx, u_ctx, conv_w, conv_w, conv_w, gpar, nw2)


HG = SSD_HEADS // SSD_GROUPS
XW = HG * SSD_HEADDIM


def _ssd_kernel(x_ref, b_ref, c_ref, z_ref, g_ref, xc_ref, bc_ref, gc_ref,
                wx_ref, wb_ref, wc_ref, bx_ref, bb_ref, bcb_ref, gpar_ref, dsk_ref, nw_ref, o_ref,
                padx_ref, padb_ref, xs_ref, bs_ref, cs_ref, gb_ref, yacc_ref, s_ref, *, t_lat, t_ctx):
    grp = pl.program_id(1)
    lane4 = _iota((CHUNK, XW), 1)
    lg = lane4 // SSD_HEADDIM
    ii = _iota((CHUNK, XW), 0)
    jj = lane4 % CHUNK
    gpar = gpar_ref[...]
    a_row = -jnp.exp(gpar[0:1, :])
    dtb_row = gpar[1:2, :]

    def gate_block(src_ref, t):
        lane = _iota((ROW_BLOCK, LANE), 1)
        ric = _iota((ROW_BLOCK, LANE), 0) % CHUNK
        shift = (LANE - HG * grp) % LANE

        def body(i, c):
            r0 = pl.multiple_of(i * ROW_BLOCK, ROW_BLOCK)
            x = src_ref[pl.ds(r0, ROW_BLOCK), :]
            dt = _softplus(x + dtb_row)
            a = pltpu.roll(dt * a_row, 2 * SSD_HEADS, 1)
            pre, suf = _chunk_cumsums(a, ric)
            blk = jnp.where(lane < 2 * SSD_HEADS, dt, jnp.where(lane < 3 * SSD_HEADS, pre, suf))
            gb_ref[pl.ds(r0, ROW_BLOCK), :] = pltpu.roll(blk, shift, 1)
            return c

        lax.fori_loop(0, t // ROW_BLOCK, body, 0)

    def expand(tile, c0):
        out = tile[:, c0 + HG - 1:c0 + HG]
        for h in range(HG - 2, -1, -1):
            out = jnp.where(lg == h, tile[:, c0 + h:c0 + h + 1], out)
        return out

    def rowsel(gt2, c0):
        out = gt2[c0 + HG - 1:c0 + HG, :]
        for h in range(HG - 2, -1, -1):
            out = jnp.where(lg[0:1, :] == h, gt2[c0 + h:c0 + h + 1, :], out)
        return out

    def step(n, d, with_out):
        r0 = pl.multiple_of(n * CHUNK, CHUNK)
        g = gb_ref[pl.ds(r0, CHUNK), :]
        gt = jnp.concatenate([g, g], axis=0).T
        gt2 = jnp.concatenate([gt, gt], axis=1)
        dtx = expand(g, SSD_HEADS * d)
        ax = expand(g, 2 * SSD_HEADS + SSD_HEADS * d)
        al = ax[CHUNK - 1:CHUNK, :] if d == 0 else ax[0:1, :]
        x = xs_ref[pl.ds(r0, CHUNK), :]
        bm = bs_ref[pl.ds(r0, CHUNK), :]
        s = s_ref[d]
        if with_out:
            cm = cs_ref[pl.ds(r0, CHUNK), :].astype(BF16)
            arow = rowsel(gt2, 2 * SSD_HEADS + SSD_HEADS * d)
            dtrow = rowsel(gt2, SSD_HEADS * d)
            incl = (ii >= jj) if d == 0 else (ii <= jj)
            cb = _dot_nt(cm, jnp.concatenate([bm] * HG, axis=0).astype(BF16))
            m = cb * jnp.exp(jnp.where(incl, ax - arow, NEG)) * dtrow
            y = _dot(m.astype(BF16), _stack_masked(x, HG).astype(BF16))
            y = y + _dot(cm, s.astype(BF16)) * jnp.exp(ax)
            yacc_ref[pl.ds(r0, CHUNK), :] += y
        xw = (x * (jnp.exp(al - ax) * dtx)).astype(BF16)
        bt = jnp.concatenate([bm, jnp.zeros_like(bm)], axis=0).T.astype(BF16)
        s_ref[d] = s * jnp.exp(al) + _dot(bt, jnp.concatenate([xw, jnp.zeros_like(xw)], axis=0))

    def phase(t, x_src, b_src, c_src, g_src, with_out):
        nc = t // CHUNK
        _conv_silu(x_src, t, padx_ref, wx_ref, bx_ref[...], xs_ref)
        _conv_silu(b_src, t, padb_ref, wb_ref, bb_ref[...], bs_ref)
        if with_out:
            _conv_silu(c_src, t, padb_ref, wc_ref, bcb_ref[...], cs_ref)
        gate_block(g_src, t)

        def body(i, c):
            step(i, 0, with_out)
            step(nc - 1 - i, 1, with_out)
            return c

        lax.fori_loop(0, nc, body, 0)

    s_ref[...] = jnp.zeros(s_ref.shape, F32)
    yacc_ref[...] = jnp.zeros(yacc_ref.shape, F32)
    phase(t_ctx, xc_ref, bc_ref, None, gc_ref, False)
    phase(t_lat, x_ref, b_ref, c_ref, g_ref, True)

    dsk = dsk_ref[...]
    nw = nw_ref[...]

    def out_body(i, c):
        r0 = pl.multiple_of(i * ROW_BLOCK, ROW_BLOCK)
        y = yacc_ref[pl.ds(r0, ROW_BLOCK), :] + dsk * xs_ref[pl.ds(r0, ROW_BLOCK), :]
        y = y * _silu(z_ref[pl.ds(r0, ROW_BLOCK), :])
        ms = jnp.mean(y * y, axis=-1, keepdims=True)
        o_ref[pl.ds(r0, ROW_BLOCK), :] = (y * lax.rsqrt(ms + EPS) * nw).astype(o_ref.dtype)
        return c

    lax.fori_loop(0, t_lat // ROW_BLOCK, out_body, 0)


def _ssd(u_lat, u_ctx, conv_w, conv_b, gpar, dskip, nw):
    bn, t, _ = u_lat.shape
    tc = u_ctx.shape[1]
    x_blk = lambda tt: pl.BlockSpec((None, tt, XW), lambda b, g: (b, 0, g))
    n_blk = lambda tt, off: pl.BlockSpec((None, tt, LANE), lambda b, g: (b, 0, off + g))
    b_off = SSD_INNER // LANE
    c_off = b_off + SSD_BC // LANE
    z_off = SSD_CONV_CH // XW
    dt_blk = (SSD_CONV_CH + SSD_INNER) // LANE
    fixed = lambda b, g: (b, 0, dt_blk)
    return pl.pallas_call(
        functools.partial(_ssd_kernel, t_lat=t, t_ctx=tc),
        grid=(bn, SSD_GROUPS),
        in_specs=[x_blk(t), n_blk(t, b_off), n_blk(t, c_off),
                  pl.BlockSpec((None, t, XW), lambda b, g: (b, 0, z_off + g)),
                  pl.BlockSpec((None, t, LANE), fixed),
                  x_blk(tc), n_blk(tc, b_off), pl.BlockSpec((None, tc, LANE), fixed),
                  pl.BlockSpec((CONV_W, XW), lambda b, g: (0, g)),
                  pl.BlockSpec((CONV_W, LANE), lambda b, g: (0, b_off + g)),
                  pl.BlockSpec((CONV_W, LANE), lambda b, g: (0, c_off + g)),
                  pl.BlockSpec((1, XW), lambda b, g: (0, g)),
                  pl.BlockSpec((1, LANE), lambda b, g: (0, b_off + g)),
                  pl.BlockSpec((1, LANE), lambda b, g: (0, c_off + g)),
                  pl.BlockSpec((SUBLANE, LANE), lambda b, g: (0, 0)),
                  pl.BlockSpec((None, 1, XW), lambda b, g: (g, 0, 0)),
                  pl.BlockSpec((None, 1, XW), lambda b, g: (g, 0, 0))],
        out_specs=pl.BlockSpec((None, t, XW), lambda b, g: (b, 0, g)),
        out_shape=jax.ShapeDtypeStruct((bn, t, SSD_INNER), BF16),
        scratch_shapes=[pltpu.VMEM((t + 2 * SUBLANE, XW), F32),
                        pltpu.VMEM((t + 2 * SUBLANE, LANE), F32),
                        pltpu.VMEM((t, XW), F32),
                        pltpu.VMEM((t, LANE), F32),
                        pltpu.VMEM((t, LANE), F32),
                        pltpu.VMEM((t, LANE), F32),
                        pltpu.VMEM((t, XW), F32),
                        pltpu.VMEM((2, SSD_STATE, XW), F32)],
        compiler_params=pltpu.CompilerParams(dimension_semantics=("arbitrary", "arbitrary"),
                                             vmem_limit_bytes=VMEM_LIMIT),
        name="ssd",
    )(u_lat, u_lat, u_lat, u_lat, u_lat, u_ctx, u_ctx, u_ctx,
      conv_w, conv_w, conv_w, conv_b, conv_b, conv_b, gpar, dskip, nw)


def _gdn_gate_params(a_log, dt_bias):
    z = jnp.zeros((SUBLANE, LANE), F32)
    z = z.at[0, 2 * GDN_HEADS:4 * GDN_HEADS].set(a_log.reshape(-1))
    return z.at[1, 2 * GDN_HEADS:4 * GDN_HEADS].set(dt_bias.reshape(-1))


def _ssd_gate_params(a_log, dt_bias):
    z = jnp.zeros((SUBLANE, LANE), F32)
    z = z.at[0, 0:2 * SSD_HEADS].set(a_log.reshape(-1))
    return z.at[1, 0:2 * SSD_HEADS].set(dt_bias.reshape(-1))


MIX_TM = 256
TOK_TILE = 128


def _mix_kernel(go_ref, so_ref, x_ref, g1_ref, sh2_ref, sc2_ref, npm_ref, npf_ref, wo_ref, wr_ref, br_ref,
                x1_ref, h2_ref, et_ref, gt_ref):
    so = so_ref[...]
    rows = MIX_TM // GRID_W
    ssd = jnp.concatenate([so[:, r * SSD_INNER:(r + 1) * SSD_INNER] for r in range(rows)], axis=0)
    m = _dot(go_ref[...], wo_ref[0:GDN_V, :]) + _dot(ssd, wo_ref[GDN_V:GDN_V + SSD_INNER, :])
    ms = jnp.mean(m * m, axis=-1, keepdims=True)
    x1 = x_ref[...] + g1_ref[...] * (m * lax.rsqrt(ms + EPS) * npm_ref[...])
    x1_ref[...] = x1
    h2 = _modulated_norm(x1, npf_ref[...], sh2_ref[...], sc2_ref[...])
    h2_ref[...] = h2
    logits = _dot_f32(h2, wr_ref[...]) + br_ref[...]
    lane = _iota(logits.shape, 1)
    lane_f = lane.astype(F32)
    l = jnp.where(lane < N_EXPERTS, logits, NEG)
    vals = []
    for _ in range(TOP_K):
        mk = jnp.max(l, axis=1, keepdims=True)
        idx = jnp.min(jnp.where(l == mk, lane_f, float(LANE)), axis=1, keepdims=True)
        l = jnp.where(lane_f == idx, NEG, l)
        vals.append(mk)
    ex = [jnp.exp(v - vals[0]) for v in vals]
    den = ex[0] + ex[1] + ex[2] + ex[3]
    gates = jnp.zeros(logits.shape, F32)
    for k in range(TOP_K):
        gates = jnp.where(lane == k, ex[k] / den, gates)
    gt_ref[...] = gates
    lt = logits.T[0:N_EXPERTS, :]
    row_f = _iota(lt.shape, 0).astype(F32)
    ids = []
    for _ in range(TOP_K):
        mk = jnp.max(lt, axis=0, keepdims=True)
        idx = jnp.min(jnp.where(lt == mk, row_f, float(N_EXPERTS)), axis=0, keepdims=True)
        lt = jnp.where(row_f == idx, NEG, lt)
        ids.append(idx.astype(I32))
    pad = jnp.full((SUBLANE - TOP_K, lt.shape[1]), -1, I32)
    et_ref[...] = jnp.concatenate(ids + [pad], axis=0)


def _mix(gdn_o, ssd_o, x, mod3, npm, npf, w_out_b16, w_router_pad, b_router_pad):
    bn, t, d = x.shape
    n = bn * t
    rows = MIX_TM // GRID_W
    nt = t // MIX_TM
    ssd_v = ssd_o.reshape(bn, GRID_W, (t // GRID_W) * SSD_INNER)
    modrow = lambda blk: pl.BlockSpec((None, 1, d), lambda b, i: (b, 0, blk))
    return pl.pallas_call(
        _mix_kernel,
        grid=(bn, nt),
        in_specs=[pl.BlockSpec((None, MIX_TM, GDN_V), lambda b, i: (b, i, 0)),
                  pl.BlockSpec((None, GRID_W, rows * SSD_INNER), lambda b, i: (b, 0, i)),
                  pl.BlockSpec((None, MIX_TM, d), lambda b, i: (b, i, 0)),
                  modrow(2), modrow(3), modrow(4),
                  pl.BlockSpec((1, d), lambda b, i: (0, 0)),
                  pl.BlockSpec((1, d), lambda b, i: (0, 0)),
                  pl.BlockSpec((GDN_V + SSD_INNER, d), lambda b, i: (0, 0)),
                  pl.BlockSpec((d, LANE), lambda b, i: (0, 0)),
                  pl.BlockSpec((1, LANE), lambda b, i: (0, 0))],
        out_specs=[pl.BlockSpec((None, MIX_TM, d), lambda b, i: (b, i, 0)),
                   pl.BlockSpec((None, MIX_TM, d), lambda b, i: (b, i, 0)),
                   pl.BlockSpec((SUBLANE, MIX_TM), lambda b, i: (0, b * nt + i)),
                   pl.BlockSpec((None, MIX_TM, LANE), lambda b, i: (b, i, 0))],
        out_shape=[jax.ShapeDtypeStruct((bn, t, d), F32),
                   jax.ShapeDtypeStruct((bn, t, d), F32),
                   jax.ShapeDtypeStruct((SUBLANE, n), I32),
                   jax.ShapeDtypeStruct((bn, t, LANE), F32)],
        compiler_params=pltpu.CompilerParams(dimension_semantics=("arbitrary", "arbitrary"),
                                             vmem_limit_bytes=VMEM_LIMIT),
        name="mix",
    )(gdn_o, ssd_v, x, mod3, mod3, mod3, npm, npf, w_out_b16, w_router_pad, b_router_pad)


def _rank_kernel(ids_ref, dest_ref, meta_ref, tril_ref, cnt_ref, rank_ref, *, n_blocks):
    nrow = ids_ref.shape[0]
    ids = ids_ref[...]
    tril_ref[...] = (_iota((nrow, nrow), 0) > _iota((nrow, nrow), 1)).astype(BF16)
    upper = (_iota((LANE, LANE), 0) < _iota((LANE, LANE), 1)).astype(BF16)
    ones = jnp.ones((LANE, LANE), BF16)
    rank_ref[...] = jnp.zeros(rank_ref.shape, F32)

    def count(e, c):
        hit = ids == e
        mask = hit.astype(F32).astype(BF16)
        before = _dot(mask, upper)
        rowsum = _dot(mask, ones)
        rows_before = _dot(tril_ref[...], rowsum.astype(BF16))
        rank_ref[...] += jnp.where(hit, before + rows_before, 0.0)
        cnt_ref[pl.ds(e, 1), :] = rows_before[nrow - 1:nrow, :] + rowsum[nrow - 1:nrow, :]
        return c

    lax.fori_loop(0, N_EXPERTS, count, 0)
    cnt = cnt_ref[...]
    padded = jnp.floor((cnt + (MOE_BLOCK - 1)) * (1.0 / MOE_BLOCK)) * MOE_BLOCK
    erow = _iota(cnt.shape, 0)
    pad_end = padded
    s = 1
    while s < N_EXPERTS:
        pad_end = pad_end + jnp.where(erow >= s, pltpu.roll(pad_end, s, 0), 0.0)
        s *= 2
    cnt_ref[...] = pad_end - padded
    jblk = (_iota((SUBLANE, LANE), 0) * LANE + _iota((SUBLANE, LANE), 1)).astype(F32) * MOE_BLOCK
    block_e = jnp.zeros((SUBLANE, LANE), F32)
    for e in range(N_EXPERTS):
        block_e = block_e + (pad_end[e:e + 1, :] <= jblk).astype(F32)
    block_e = jnp.minimum(block_e, N_EXPERTS - 1.0)
    n_used = jnp.broadcast_to(pad_end[N_EXPERTS - 1:N_EXPERTS, :] * (1.0 / MOE_BLOCK), (SUBLANE, LANE))
    meta_ref[...] = jnp.concatenate([block_e, n_used], axis=0).astype(I32)

    def place(e, c):
        rank_ref[...] += jnp.where(ids == e, cnt_ref[pl.ds(e, 1), :], 0.0)
        return c

    lax.fori_loop(0, N_EXPERTS, place, 0)
    dest_ref[...] = rank_ref[...].astype(I32)


def _rank(ids2, n_blocks):
    nrow = ids2.shape[0]
    return pl.pallas_call(
        functools.partial(_rank_kernel, n_blocks=n_blocks),
        in_specs=[pl.BlockSpec((nrow, LANE), lambda: (0, 0))],
        out_specs=[pl.BlockSpec((nrow, LANE), lambda: (0, 0)),
                   pl.BlockSpec((2 * SUBLANE, LANE), lambda: (0, 0))],
        out_shape=[jax.ShapeDtypeStruct((nrow, LANE), I32),
                   jax.ShapeDtypeStruct((2 * SUBLANE, LANE), I32)],
        scratch_shapes=[pltpu.VMEM((nrow, nrow), BF16),
                        pltpu.VMEM((N_EXPERTS, LANE), F32),
                        pltpu.VMEM((nrow, LANE), F32)],
        compiler_params=pltpu.CompilerParams(vmem_limit_bytes=VMEM_LIMIT),
        name="rank",
    )(ids2)


def _row_copy(src, s, dst, d, sem):
    return pltpu.make_async_copy(src.at[pl.ds(s, 1), :], dst.at[pl.ds(d, 1), :], sem)


def _dispatch_kernel(dest_ref, h_ref, xg_in_ref, xg_ref, sem):
    del xg_in_ref
    base = pl.program_id(0) * TOK_TILE

    def issue(t, c):
        for k in range(TOP_K):
            _row_copy(h_ref, base + t, xg_ref, dest_ref[k, t], sem).start()
        return c

    lax.fori_loop(0, TOK_TILE, issue, 0)

    def drain(t, c):
        for k in range(TOP_K):
            _row_copy(h_ref, 0, xg_ref, 0, sem).wait()
        return c

    lax.fori_loop(0, TOK_TILE, drain, 0)


def _dispatch(dest3, h2, xg0):
    ntile = dest3.shape[0]
    return pl.pallas_call(
        _dispatch_kernel,
        grid=(ntile,),
        in_specs=[pl.BlockSpec((None, TOP_K, TOK_TILE), lambda i: (i, 0, 0), memory_space=pltpu.SMEM),
                  pl.BlockSpec(memory_space=pl.ANY),
                  pl.BlockSpec(memory_space=pl.ANY)],
        out_specs=pl.BlockSpec(memory_space=pl.ANY),
        out_shape=jax.ShapeDtypeStruct(xg0.shape, xg0.dtype),
        scratch_shapes=[pltpu.SemaphoreType.DMA(())],
        input_output_aliases={2: 0},
        compiler_params=pltpu.CompilerParams(dimension_semantics=("arbitrary",)),
        name="dispatch",
    )(dest3, h2, xg0)


def _experts_kernel(be_ref, nu_ref, x_ref, wg_ref, wu_ref, wd_ref, bg_ref, bu_ref, bd_ref, y_ref):
    j = pl.program_id(0)

    @pl.when(j < nu_ref[0])
    def _():
        x = x_ref[...].astype(BF16)
        a = jnp.minimum(_dot(x, wg_ref[...]) + bg_ref[...], SWIGLU_LIMIT)
        b = jnp.clip(_dot(x, wu_ref[...]) + bu_ref[...], -SWIGLU_LIMIT, SWIGLU_LIMIT)
        hid = a * _sigmoid(SWIGLU_ALPHA * a) * (b + 1.0)
        y_ref[...] = _dot(hid.astype(BF16), wd_ref[...]) + bd_ref[...]

    @pl.when(j >= nu_ref[0])
    def _():
        y_ref[...] = jnp.zeros(y_ref.shape, y_ref.dtype)


def _experts(block_e, n_used, xg, wg, wu, wd, bg, bu, bd):
    cap, d = xg.shape
    n_blocks = cap // MOE_BLOCK
    de = wg.shape[2]
    wspec = lambda a, b: pl.BlockSpec((None, a, b), lambda j, be, nu: (be[j], 0, 0))
    grid_spec = pltpu.PrefetchScalarGridSpec(
        num_scalar_prefetch=2,
        grid=(n_blocks,),
        in_specs=[pl.BlockSpec((MOE_BLOCK, d), lambda j, be, nu: (jnp.minimum(j, nu[0] - 1), 0)),
                  wspec(d, de), wspec(d, de), wspec(de, d),
                  wspec(1, de), wspec(1, de), wspec(1, d)],
        out_specs=pl.BlockSpec((MOE_BLOCK, d), lambda j, be, nu: (j, 0)),
    )
    return pl.pallas_call(
        _experts_kernel,
        grid_spec=grid_spec,
        out_shape=jax.ShapeDtypeStruct((cap, d), F32),
        compiler_params=pltpu.CompilerParams(dimension_semantics=("arbitrary",),
                                             vmem_limit_bytes=VMEM_LIMIT),
        name="experts",
    )(block_e, n_used, xg, wg, wu, wd, bg, bu, bd)


def _combine_kernel(dest_ref, yg_ref, gt_ref, x1_ref, g2_ref, nw_ref, o_ref, ybuf, sem):
    def issue(t, c):
        for k in range(TOP_K):
            pltpu.make_async_copy(yg_ref.at[pl.ds(dest_ref[k, t], 1), :], ybuf.at[k, pl.ds(t, 1), :], sem).start()
        return c

    lax.fori_loop(0, TOK_TILE, issue, 0)

    def drain(t, c):
        for k in range(TOP_K):
            pltpu.make_async_copy(yg_ref.at[pl.ds(0, 1), :], ybuf.at[k, pl.ds(t, 1), :], sem).wait()
        return c

    lax.fori_loop(0, TOK_TILE, drain, 0)
    gt = gt_ref[...]
    f = gt[:, 0:1] * ybuf[0]
    for k in range(1, TOP_K):
        f = f + gt[:, k:k + 1] * ybuf[k]
    ms = jnp.mean(f * f, axis=-1, keepdims=True)
    o_ref[...] = x1_ref[...] + g2_ref[...] * (f * lax.rsqrt(ms + EPS) * nw_ref[...])


def _combine(dest3, yg, gates_tok, x1, mod3, npf2, t_per_batch):
    n, d = x1.shape
    ntile = n // TOK_TILE
    per_b = t_per_batch // TOK_TILE
    return pl.pallas_call(
        _combine_kernel,
        grid=(ntile,),
        in_specs=[pl.BlockSpec((None, TOP_K, TOK_TILE), lambda i: (i, 0, 0), memory_space=pltpu.SMEM),
                  pl.BlockSpec(memory_space=pl.ANY),
                  pl.BlockSpec((TOK_TILE, LANE), lambda i: (i, 0)),
                  pl.BlockSpec((TOK_TILE, d), lambda i: (i, 0)),
                  pl.BlockSpec((None, 1, d), lambda i: (i // per_b, 0, 5)),
                  pl.BlockSpec((1, d), lambda i: (0, 0))],
        out_specs=pl.BlockSpec((TOK_TILE, d), lambda i: (i, 0)),
        out_shape=jax.ShapeDtypeStruct((n, d), F32),
        scratch_shapes=[pltpu.VMEM((TOP_K, TOK_TILE, d), F32),
                        pltpu.SemaphoreType.DMA(())],
        compiler_params=pltpu.CompilerParams(dimension_semantics=("arbitrary",),
                                             vmem_limit_bytes=VMEM_LIMIT),
        name="combine",
    )(dest3, yg, gates_tok, x1, mod3, npf2)


def _pad_cols(w, cols):
    return jnp.pad(w, ((0, 0), (0, cols - w.shape[1])))


def kernel(x, c, ctx, c_ctx, w_ada, b_ada, norm_pre_mix, norm_post_mix, norm_pre_ffn, norm_post_ffn,
           w_in, gdn_conv_w, gdn_A_log, gdn_dt_bias, gdn_norm_w, ssd_conv_w, ssd_conv_b, ssd_A_log,
           ssd_dt_bias, ssd_D, ssd_norm_w, w_out, w_router, b_router, w_gate, b_gate, w_up, b_up,
           w_down, b_down):
    bn, t, d = x.shape
    n = bn * t
    l = 0
    mod_rows = -(-(bn + 1) // SUBLANE) * SUBLANE
    cc = jnp.zeros((mod_rows, d), F32).at[:bn].set(c).at[bn].set(c_ctx)
    mod3 = _ada(cc, w_ada[l], b_ada[l]).reshape(mod_rows, 1, 6 * d)

    w_in_b = w_in[l].astype(BF16)
    w_gdn = _pad_cols(w_in_b[:, :GDN_COLS], GDN_COLS_PAD)
    w_ssd = _pad_cols(w_in_b[:, GDN_COLS:], SSD_COLS_PAD)
    npm1 = norm_pre_mix[l].reshape(1, d)
    ug = _inproj(x, mod3, npm1, w_gdn, mod_row=None, col_major=False)
    us = _inproj(x, mod3, npm1, w_ssd, mod_row=None, col_major=True)
    ugc = _inproj(ctx, mod3, npm1, w_gdn, mod_row=bn, col_major=False)
    usc = _inproj(ctx, mod3, npm1, w_ssd, mod_row=bn, col_major=False)

    gdn_o = _gdn(ug, ugc, gdn_conv_w[l], _gdn_gate_params(gdn_A_log[l], gdn_dt_bias[l]),
                 jnp.tile(gdn_norm_w[l], 2).reshape(1, LANE))
    ssd_o = _ssd(us, usc, ssd_conv_w[l], ssd_conv_b[l].reshape(1, SSD_CONV_CH),
                 _ssd_gate_params(ssd_A_log[l], ssd_dt_bias[l]),
                 jnp.repeat(ssd_D[l], SSD_HEADDIM).reshape(SSD_GROUPS, 1, XW),
                 ssd_norm_w[l].reshape(SSD_GROUPS, 1, XW))

    x1, h2, e_t, gates_tok = _mix(
        gdn_o, ssd_o, x, mod3, norm_post_mix[l].reshape(1, d), norm_pre_ffn[l].reshape(1, d),
        w_out[l].astype(BF16), _pad_cols(w_router[l], LANE), _pad_cols(b_router[l].reshape(1, N_EXPERTS), LANE))

    n_blocks = -(-(n * TOP_K) // MOE_BLOCK) + N_EXPERTS
    cap = n_blocks * MOE_BLOCK
    ntile = n // TOK_TILE
    ids2 = e_t.reshape(SUBLANE * ntile, LANE)[:TOP_K * ntile]
    dest, meta = _rank(ids2, n_blocks)
    dest3 = dest.reshape(TOP_K, ntile, LANE).transpose(1, 0, 2)
    block_e = meta[:SUBLANE].reshape(-1)[:n_blocks]
    n_used = meta[SUBLANE, :1]

    xg = _dispatch(dest3, h2.reshape(n, d), jnp.zeros((cap, d), F32))
    yg = _experts(block_e, n_used, xg,
                  w_gate[l].astype(BF16), w_up[l].astype(BF16), w_down[l].astype(BF16),
                  b_gate[l].reshape(N_EXPERTS, 1, D_EXPERT), b_up[l].reshape(N_EXPERTS, 1, D_EXPERT),
                  b_down[l].reshape(N_EXPERTS, 1, d))
    out = _combine(dest3, yg, gates_tok.reshape(n, LANE), x1.reshape(n, d), mod3,
                   norm_post_ffn[l].reshape(1, d), t)
    return out.reshape(bn, t, d)
```

```python
import functools

import jax
import jax.numpy as jnp
from jax import lax
from jax.experimental import pallas as pl
from jax.experimental.pallas import tpu as pltpu

F32 = jnp.float32
BF16 = jnp.bfloat16
I32 = jnp.int32

D_MODEL = 1024
GRID_W = 64
GDN_HEADS = 8
GDN_DK = 64
GDN_DV = 64
SSD_HEADS = 8
SSD_HEADDIM = 64
SSD_GROUPS = 2
SSD_STATE = 128
CONV_W = 5
CHUNK = 64
N_EXPERTS = 32
TOP_K = 4
D_EXPERT = 1024
SWIGLU_LIMIT = 7.0
SWIGLU_ALPHA = 1.702
MOE_BLOCK = 256
EPS = 1e-6

GDN_QK = GDN_HEADS * GDN_DK
GDN_V = GDN_HEADS * GDN_DV
GDN_CONV_CH = 2 * GDN_QK + GDN_V
GDN_COLS = GDN_CONV_CH + GDN_V + 4 * GDN_HEADS
SSD_INNER = SSD_HEADS * SSD_HEADDIM
SSD_BC = SSD_GROUPS * SSD_STATE
SSD_CONV_CH = SSD_INNER + 2 * SSD_BC
SSD_COLS = SSD_CONV_CH + SSD_INNER + 2 * SSD_HEADS

LANE = 128
SUBLANE = 8
GDN_COLS_PAD = 17 * LANE
SSD_COLS_PAD = 13 * LANE
VMEM_LIMIT = 56 * 1024 * 1024
NEG = -1e30
ROW_BLOCK = 256
INTRA_GROUP = 4
SSD_GROUP = 2


def _dot(a, b):
    return jnp.dot(a, b, preferred_element_type=F32)


def _dot_nt(a, b):
    return lax.dot_general(a, b, (((1,), (1,)), ((), ())), preferred_element_type=F32)


def _dot_f32(a, b):
    return jnp.dot(a, b, preferred_element_type=F32, precision=lax.Precision.HIGHEST)


def _silu(x):
    return x * (1.0 / (1.0 + jnp.exp(-x)))


def _sigmoid(x):
    return 1.0 / (1.0 + jnp.exp(-x))


def _softplus(x):
    return jnp.maximum(x, 0.0) + jnp.log(1.0 + jnp.exp(-jnp.abs(x)))


def _iota(shape, axis):
    return lax.broadcasted_iota(I32, shape, axis)


def _split_dot(x, ones_b16):
    hi = x.astype(BF16)
    lo = (x - hi.astype(F32)).astype(BF16)
    return _dot(hi, ones_b16) + _dot(lo, ones_b16)


def _ada_kernel(c_ref, w_ref, b_ref, o_ref):
    o_ref[...] = _dot_f32(_silu(c_ref[...]), w_ref[...]) + b_ref[...]


def _ada(cc, w_ada, b_ada):
    rows, d = cc.shape
    n = w_ada.shape[1]
    tn = 1024
    return pl.pallas_call(
        _ada_kernel,
        grid=(n // tn,),
        in_specs=[pl.BlockSpec((rows, d), lambda j: (0, 0)),
                  pl.BlockSpec((d, tn), lambda j: (0, j)),
                  pl.BlockSpec((1, tn), lambda j: (0, j))],
        out_specs=pl.BlockSpec((rows, tn), lambda j: (0, j)),
        out_shape=jax.ShapeDtypeStruct((rows, n), F32),
        compiler_params=pltpu.CompilerParams(dimension_semantics=("arbitrary",),
                                             vmem_limit_bytes=VMEM_LIMIT),
        name="ada",
    )(cc, w_ada, b_ada.reshape(1, n))


def _modulated_norm(x, nw, sh, sc):
    ms = jnp.mean(x * x, axis=-1, keepdims=True)
    return (x * lax.rsqrt(ms + EPS) * nw) * (1.0 + sc) + sh


def _inproj_kernel(x_ref, sh_ref, sc_ref, nw_ref, w_ref, o_ref, *, pieces):
    if pieces:
        xt = x_ref[...]
        x = jnp.concatenate([xt[:, i * D_MODEL:(i + 1) * D_MODEL] for i in range(pieces)], axis=0)
    else:
        x = x_ref[...]
    h = _modulated_norm(x, nw_ref[...], sh_ref[...], sc_ref[...])
    o_ref[...] = _dot(h.astype(BF16), w_ref[...])


def _inproj(x, mod3, nw, w_b16, *, mod_row, col_major):
    bn, t, d = x.shape
    cols = w_b16.shape[1]
    if col_major:
        rows = t // GRID_W
        pieces = 8
        tm = pieces * rows
        xin = x.reshape(bn, rows, GRID_W * d)
        x_spec = pl.BlockSpec((None, rows, pieces * d), lambda b, i: (b, 0, i))
    else:
        pieces = 0
        tm = min(512, t)
        xin = x
        x_spec = pl.BlockSpec((None, tm, d), lambda b, i: (b, i, 0))
    if mod_row is None:
        row = lambda b: b
    else:
        row = lambda b: mod_row
    return pl.pallas_call(
        functools.partial(_inproj_kernel, pieces=pieces),
        grid=(bn, t // tm),
        in_specs=[x_spec,
                  pl.BlockSpec((None, 1, d), lambda b, i: (row(b), 0, 0)),
                  pl.BlockSpec((None, 1, d), lambda b, i: (row(b), 0, 1)),
                  pl.BlockSpec((1, d), lambda b, i: (0, 0)),
                  pl.BlockSpec((d, cols), lambda b, i: (0, 0))],
        out_specs=pl.BlockSpec((None, tm, cols), lambda b, i: (b, i, 0)),
        out_shape=jax.ShapeDtypeStruct((bn, t, cols), F32),
        compiler_params=pltpu.CompilerParams(dimension_semantics=("arbitrary", "arbitrary"),
                                             vmem_limit_bytes=VMEM_LIMIT),
        name="inproj_cm" if col_major else "inproj",
    )(xin, mod3, mod3, nw, w_b16)


def _conv_silu(src_ref, t, pad_ref, w_ref, bias, dst_ref, post=None):
    width = src_ref.shape[-1]
    zero = jnp.zeros((SUBLANE, width), F32)
    pad_ref[0:SUBLANE, :] = zero
    pad_ref[pl.ds(SUBLANE + t, SUBLANE), :] = zero
    rb = ROW_BLOCK * LANE // width
    nb = t // rb

    def copy(i, c):
        r0 = pl.multiple_of(i * rb, rb)
        pad_ref[pl.ds(SUBLANE + r0, rb), :] = src_ref[pl.ds(r0, rb), :]
        return c

    lax.fori_loop(0, nb, copy, 0)
    w = w_ref[...]

    def body(i, c):
        r0 = pl.multiple_of(i * rb, rb)
        win = pad_ref[pl.ds(r0, rb + 2 * SUBLANE), :]
        acc = None
        for j in range(CONV_W):
            off = SUBLANE - CONV_W // 2 + j
            tap = pltpu.roll(win, rb + 2 * SUBLANE - off, 0)[0:rb, :] * w[j:j + 1, :]
            acc = tap if acc is None else acc + tap
        if bias is not None:
            acc = acc + bias
        y = _silu(acc)
        if post is not None:
            y = post(y)
        dst_ref[pl.ds(r0, rb), :] = y
        return c

    lax.fori_loop(0, nb, body, 0)


def _chunk_cumsums(x, rows_in_chunk):
    n = x.shape[0]
    pre = x
    suf = x
    s = 1
    while s < CHUNK:
        pre = pre + jnp.where(rows_in_chunk >= s, pltpu.roll(pre, s, 0), 0.0)
        suf = suf + jnp.where(rows_in_chunk < CHUNK - s, pltpu.roll(suf, n - s, 0), 0.0)
        s *= 2
    return pre, suf


def _stack_masked(x, groups):
    r, n = x.shape
    w = n // groups
    lane_group = _iota((r, n), 1) // w
    return jnp.concatenate([jnp.where(lane_group == g, x, 0.0) for g in range(groups)], axis=0)


def _gdn_kernel(q_ref, k_ref, v_ref, z_ref, g_ref, kc_ref, vc_ref, gc_ref,
                wq_ref, wk_ref, wv_ref, gpar_ref, nw_ref, o_ref,
                pad_ref, qn_ref, kn_ref, vv_ref, gb_ref, u_ref, w_ref, at_ref, qe_ref,
                kdt_ref, egl_ref, oacc_ref, s_ref, *, t_lat, t_ctx):
    hp = pl.program_id(1)
    ones_bd = (_iota((LANE, LANE), 0) // GDN_DK == _iota((LANE, LANE), 1) // GDN_DK).astype(BF16)
    lane64 = _iota((CHUNK, LANE), 1)
    first = lane64 < GDN_DK
    ii = _iota((CHUNK, LANE), 0)
    jj = lane64 % CHUNK
    eye = (ii == jj).astype(F32)
    bd_mask = (_iota((LANE, LANE), 0) // GDN_DK) == (_iota((LANE, LANE), 1) // GDN_DV)
    gpar = gpar_ref[...]
    a_row = jnp.exp(gpar[0:1, :])
    dtb_row = gpar[1:2, :]

    def l2norm(scale):
        def f(y):
            ss = _split_dot(y * y, ones_bd)
            return y * lax.rsqrt(ss + EPS) * scale
        return f

    def gate_block(src_ref, t):
        lane = _iota((ROW_BLOCK, LANE), 1)
        ric = _iota((ROW_BLOCK, LANE), 0) % CHUNK
        shift = (LANE - 2 * hp) % LANE

        def body(i, c):
            r0 = pl.multiple_of(i * ROW_BLOCK, ROW_BLOCK)
            x = src_ref[pl.ds(r0, ROW_BLOCK), :]
            act = jnp.where(lane < 2 * GDN_HEADS, _sigmoid(x), -a_row * _softplus(x + dtb_row))
            pre, suf = _chunk_cumsums(act, ric)
            blk = jnp.where(lane < 2 * GDN_HEADS, act, jnp.where(lane < 3 * GDN_HEADS, pre, suf))
            gb_ref[pl.ds(r0, ROW_BLOCK), :] = pltpu.roll(blk, shift, 1)
            return c

        lax.fori_loop(0, t // ROW_BLOCK, body, 0)

    def expand(tile, c0):
        return jnp.where(first, tile[:, c0:c0 + 1], tile[:, c0 + 1:c0 + 2])

    first2 = jnp.concatenate([first, first], axis=1)
    dirs = (0, 1)

    def intra(ns, with_out):
        r0s = [pl.multiple_of(n * CHUNK, CHUNK) for n in ns]
        ks = [kn_ref[pl.ds(r0, CHUNK), :] for r0 in r0s]
        gs = [gb_ref[pl.ds(r0, CHUNK), :] for r0 in r0s]
        gts = [jnp.concatenate([g, g], axis=0).T for g in gs]
        k_bds = [_stack_masked(k, 2).astype(BF16) for k in ks]
        probs = [(c, d) for c in range(len(ns)) for d in dirs]
        beta, gx, rel, gl, kb = {}, {}, {}, {}, {}
        for c, d in probs:
            g = gs[c]
            c0 = 2 * GDN_HEADS + GDN_HEADS * d
            beta[c, d] = expand(g, GDN_HEADS * d)
            gx[c, d] = expand(g, c0)
            grow = jnp.where(first[0:1, :], gts[c][c0:c0 + 1, :], gts[c][c0 + 1:c0 + 2, :])
            rel[c, d] = gx[c, d] - grow
            gl[c, d] = gx[c, d][CHUNK - 1:CHUNK, :] if d == 0 else gx[c, d][0:1, :]
            kb[c, d] = ks[c] * beta[c, d]
        strict = {0: ii > jj, 1: ii < jj}
        incl = {0: ii >= jj, 1: ii <= jj}
        lhs = []
        for c in range(len(ns)):
            parts = [kb[c, 0], kb[c, 1]]
            if with_out:
                parts.append(qn_ref[pl.ds(r0s[c], CHUNK), :])
            lhs.append(jnp.concatenate(parts, axis=0).astype(BF16))
        kk = [_dot_nt(lhs[c], k_bds[c]) for c in range(len(ns))]
        nm, p = {}, {}
        for c, d in probs:
            a = kk[c][d * CHUNK:(d + 1) * CHUNK, :] * jnp.exp(jnp.where(strict[d], rel[c, d], NEG))
            nm[c, d] = -a
            p[c, d] = eye + nm[c, d]
        for _ in range(5):
            for key in probs:
                nm[key] = _dot(nm[key].astype(BF16), _stack_masked(nm[key], 2).astype(BF16))
            for key in probs:
                p[key] = p[key] + _dot(p[key].astype(BF16), _stack_masked(nm[key], 2).astype(BF16))
        vs = [vv_ref[pl.ds(r0, CHUNK), :] for r0 in r0s]
        eg, uw = {}, {}
        for c, d in probs:
            eg[c, d] = jnp.exp(gx[c, d])
            rhs = jnp.concatenate([vs[c] * beta[c, d], kb[c, d] * eg[c, d]], axis=1)
            rhs_bd = jnp.concatenate([jnp.where(first2, rhs, 0.0), jnp.where(first2, 0.0, rhs)], axis=0)
            uw[c, d] = _dot(p[c, d].astype(BF16), rhs_bd.astype(BF16))
        for c, d in probs:
            r0, n = r0s[c], ns[c]
            u_ref[d, pl.ds(r0, CHUNK), :] = uw[c, d][:, :LANE]
            w_ref[d, pl.ds(r0, CHUNK), :] = uw[c, d][:, LANE:].astype(BF16)
            kdec = ks[c] * jnp.exp(gl[c, d] - gx[c, d])
            kdt = jnp.concatenate([kdec, jnp.zeros_like(kdec)], axis=0).T
            kdt_ref[d, pl.ds(pl.multiple_of(n * LANE, LANE), LANE), :] = kdt.astype(BF16)
            egl_ref[d, n] = jnp.broadcast_to(jnp.exp(gl[c, d]), (SUBLANE, LANE))
            if with_out:
                at = kk[c][2 * CHUNK:3 * CHUNK, :] * jnp.exp(jnp.where(incl[d], rel[c, d], NEG))
                at_ref[d, pl.ds(r0, CHUNK), :] = at.astype(BF16)
                qe_ref[d, pl.ds(r0, CHUNK), :] = (qn_ref[pl.ds(r0, CHUNK), :] * eg[c, d]).astype(BF16)

    def scan_step(ns, with_out):
        r0s = [pl.multiple_of(n * CHUNK, CHUNK) for n in ns]
        s = [s_ref[d] for d in dirs]
        sb = [s[d].astype(BF16) for d in dirs]
        if with_out:
            ws = [_dot(jnp.concatenate([w_ref[d, pl.ds(r0s[d], CHUNK), :], qe_ref[d, pl.ds(r0s[d], CHUNK), :]],
                                       axis=0), sb[d]) for d in dirs]
        else:
            ws = [_dot(w_ref[d, pl.ds(r0s[d], CHUNK), :], sb[d]) for d in dirs]
        vnew = [u_ref[d, pl.ds(r0s[d], CHUNK), :] - ws[d][0:CHUNK, :] for d in dirs]
        vb = [v.astype(BF16) for v in vnew]
        upd = [_dot(kdt_ref[d, pl.ds(pl.multiple_of(ns[d] * LANE, LANE), LANE), :],
                    jnp.concatenate([vb[d], jnp.zeros_like(vb[d])], axis=0)) for d in dirs]
        for d in dirs:
            s_ref[d] = s[d] * egl_ref[d, ns[d]][0:1, :] + jnp.where(bd_mask, upd[d], 0.0)
        if with_out:
            av = [_dot(at_ref[d, pl.ds(r0s[d], CHUNK), :], _stack_masked(vnew[d], 2).astype(BF16)) for d in dirs]
            for d in dirs:
                oacc_ref[pl.ds(r0s[d], CHUNK), :] += ws[d][CHUNK:2 * CHUNK, :] + av[d]

    def phase(t, k_src, v_src, g_src, q_src, with_out):
        nc = t // CHUNK
        _conv_silu(k_src, t, pad_ref, wk_ref, None, kn_ref, l2norm(1.0))
        _conv_silu(v_src, t, pad_ref, wv_ref, None, vv_ref)
        if with_out:
            _conv_silu(q_src, t, pad_ref, wq_ref, None, qn_ref, l2norm(GDN_DK ** -0.5))
        gate_block(g_src, t)

        def intra_body(i, c):
            intra([i * INTRA_GROUP + j for j in range(INTRA_GROUP)], with_out)
            return c

        lax.fori_loop(0, nc // INTRA_GROUP, intra_body, 0)

        def scan_body(i, c):
            scan_step([i, nc - 1 - i], with_out)
            return c

        lax.fori_loop(0, nc, scan_body, 0)

    s_ref[...] = jnp.zeros(s_ref.shape, F32)
    oacc_ref[...] = jnp.zeros(oacc_ref.shape, F32)
    phase(t_ctx, kc_ref, vc_ref, gc_ref, None, False)
    phase(t_lat, k_ref, v_ref, g_ref, q_ref, True)

    nw = nw_ref[...]

    def out_body(i, c):
        r0 = pl.multiple_of(i * ROW_BLOCK, ROW_BLOCK)
        o = oacc_ref[pl.ds(r0, ROW_BLOCK), :]
        ms = _split_dot(o * o, ones_bd) * (1.0 / GDN_DV)
        y = o * lax.rsqrt(ms + EPS) * nw * _silu(z_ref[pl.ds(r0, ROW_BLOCK), :])
        o_ref[pl.ds(r0, ROW_BLOCK), :] = y.astype(o_ref.dtype)
        return c

    lax.fori_loop(0, t_lat // ROW_BLOCK, out_body, 0)


def _gdn(u_lat, u_ctx, conv_w, gpar, nw2):
    bn, t, _ = u_lat.shape
    tc = u_ctx.shape[1]
    npairs = GDN_HEADS // 2
    nc = t // CHUNK
    col = lambda off: (lambda b, h: (b, 0, off + h))
    lat = lambda off: pl.BlockSpec((None, t, LANE), col(off))
    ctx = lambda off: pl.BlockSpec((None, tc, LANE), col(off))
    gate_blk = GDN_CONV_CH // LANE + GDN_V // LANE
    fixed = lambda b, h: (b, 0, gate_blk)
    cw = lambda off: pl.BlockSpec((CONV_W, LANE), lambda b, h: (0, off + h))
    return pl.pallas_call(
        functools.partial(_gdn_kernel, t_lat=t, t_ctx=tc),
        grid=(bn, npairs),
        in_specs=[lat(0), lat(npairs), lat(2 * npairs), lat(3 * npairs),
                  pl.BlockSpec((None, t, LANE), fixed),
                  ctx(npairs), ctx(2 * npairs), pl.BlockSpec((None, tc, LANE), fixed),
                  cw(0), cw(npairs), cw(2 * npairs),
                  pl.BlockSpec((SUBLANE, LANE), lambda b, h: (0, 0)),
                  pl.BlockSpec((1, LANE), lambda b, h: (0, 0))],
        out_specs=pl.BlockSpec((None, t, LANE), lambda b, h: (b, 0, h)),
        out_shape=jax.ShapeDtypeStruct((bn, t, GDN_V), BF16),
        scratch_shapes=[pltpu.VMEM((t + 2 * SUBLANE, LANE), F32),
                        pltpu.VMEM((t, LANE), F32),
                        pltpu.VMEM((t, LANE), F32),
                        pltpu.VMEM((t, LANE), F32),
                        pltpu.VMEM((t, LANE), F32),
                        pltpu.VMEM((2, t, LANE), F32),
                        pltpu.VMEM((2, t, LANE), BF16),
                        pltpu.VMEM((2, t, LANE), BF16),
                        pltpu.VMEM((2, t, LANE), BF16),
                        pltpu.VMEM((2, nc * LANE, LANE), BF16),
                        pltpu.VMEM((2, nc, SUBLANE, LANE), F32),
                        pltpu.VMEM((t, LANE), F32),
                        pltpu.VMEM((2, LANE, LANE), F32)],
        compiler_params=pltpu.CompilerParams(dimension_semantics=("arbitrary", "arbitrary"),
                                             vmem_limit_bytes=VMEM_LIMIT),
        name="gdn",
    )(u_lat, u_lat, u_lat, u_lat, u_lat, u_ctx, u_ctx, u_ctx, conv_w, conv_w, conv_w, gpar, nw2)


HG = SSD_HEADS // SSD_GROUPS
XW = HG * SSD_HEADDIM


def _ssd_kernel(x_ref, b_ref, c_ref, z_ref, g_ref, xc_ref, bc_ref, gc_ref,
                wx_ref, wb_ref, wc_ref, bx_ref, bb_ref, bcb_ref, gpar_ref, dsk_ref, nw_ref, o_ref,
                padx_ref, padb_ref, xs_ref, bs_ref, cs_ref, gb_ref, yacc_ref, s_ref, *, t_lat, t_ctx):
    grp = pl.program_id(1)
    lane4 = _iota((CHUNK, XW), 1)
    lg = lane4 // SSD_HEADDIM
    ii = _iota((CHUNK, XW), 0)
    jj = lane4 % CHUNK
    gpar = gpar_ref[...]
    a_row = -jnp.exp(gpar[0:1, :])
    dtb_row = gpar[1:2, :]

    def gate_block(src_ref, t):
        lane = _iota((ROW_BLOCK, LANE), 1)
        ric = _iota((ROW_BLOCK, LANE), 0) % CHUNK
        shift = (LANE - HG * grp) % LANE

        def body(i, c):
            r0 = pl.multiple_of(i * ROW_BLOCK, ROW_BLOCK)
            x = src_ref[pl.ds(r0, ROW_BLOCK), :]
            dt = _softplus(x + dtb_row)
            a = pltpu.roll(dt * a_row, 2 * SSD_HEADS, 1)
            pre, suf = _chunk_cumsums(a, ric)
            blk = jnp.where(lane < 2 * SSD_HEADS, dt, jnp.where(lane < 3 * SSD_HEADS, pre, suf))
            gb_ref[pl.ds(r0, ROW_BLOCK), :] = pltpu.roll(blk, shift, 1)
            return c

        lax.fori_loop(0, t // ROW_BLOCK, body, 0)

    def expand(tile, c0):
        out = tile[:, c0 + HG - 1:c0 + HG]
        for h in range(HG - 2, -1, -1):
            out = jnp.where(lg == h, tile[:, c0 + h:c0 + h + 1], out)
        return out

    def rowsel(gt2, c0):
        out = gt2[c0 + HG - 1:c0 + HG, :]
        for h in range(HG - 2, -1, -1):
            out = jnp.where(lg[0:1, :] == h, gt2[c0 + h:c0 + h + 1, :], out)
        return out

    def step(chunks, with_out):
        probs = [(d, j) for d in (0, 1) for j in range(len(chunks[d]))]
        r0, ax, al, x, bm, cm, xw, upd, yin = {}, {}, {}, {}, {}, {}, {}, {}, {}
        for key in probs:
            d, j = key
            r0[key] = pl.multiple_of(chunks[d][j] * CHUNK, CHUNK)
            g = gb_ref[pl.ds(r0[key], CHUNK), :]
            dtx = expand(g, SSD_HEADS * d)
            ax[key] = expand(g, 2 * SSD_HEADS + SSD_HEADS * d)
            al[key] = ax[key][CHUNK - 1:CHUNK, :] if d == 0 else ax[key][0:1, :]
            x[key] = xs_ref[pl.ds(r0[key], CHUNK), :]
            bm[key] = bs_ref[pl.ds(r0[key], CHUNK), :]
            xw[key] = (x[key] * (jnp.exp(al[key] - ax[key]) * dtx)).astype(BF16)
            if with_out:
                gt = jnp.concatenate([g, g], axis=0).T
                gt2 = jnp.concatenate([gt, gt], axis=1)
                arow = rowsel(gt2, 2 * SSD_HEADS + SSD_HEADS * d)
                dtrow = rowsel(gt2, SSD_HEADS * d)
                incl = (ii >= jj) if d == 0 else (ii <= jj)
                yin[key] = jnp.exp(jnp.where(incl, ax[key] - arow, NEG)) * dtrow
                cm[key] = cs_ref[pl.ds(r0[key], CHUNK), :].astype(BF16)
        if with_out:
            cb = {key: _dot_nt(cm[key], jnp.concatenate([bm[key]] * HG, axis=0).astype(BF16)) for key in probs}
        for key in probs:
            bt = jnp.concatenate([bm[key], jnp.zeros_like(bm[key])], axis=0).T.astype(BF16)
            upd[key] = _dot(bt, jnp.concatenate([xw[key], jnp.zeros_like(xw[key])], axis=0))
        if with_out:
            for key in probs:
                yin[key] = _dot((cb[key] * yin[key]).astype(BF16), _stack_masked(x[key], HG).astype(BF16))
        for d in (0, 1):
            s = s_ref[d]
            for j in range(len(chunks[d])):
                key = (d, j)
                if with_out:
                    y = yin[key] + _dot(cm[key], s.astype(BF16)) * jnp.exp(ax[key])
                    yacc_ref[pl.ds(r0[key], CHUNK), :] += y
                s = s * jnp.exp(al[key]) + upd[key]
            s_ref[d] = s

    def phase(t, x_src, b_src, c_src, g_src, with_out):
        nc = t // CHUNK
        _conv_silu(x_src, t, padx_ref, wx_ref, bx_ref[...], xs_ref)
        _conv_silu(b_src, t, padb_ref, wb_ref, bb_ref[...], bs_ref)
        if with_out:
            _conv_silu(c_src, t, padb_ref, wc_ref, bcb_ref[...], cs_ref)
        gate_block(g_src, t)

        def body(i, c):
            fwd = [i * SSD_GROUP + j for j in range(SSD_GROUP)]
            bwd = [nc - 1 - i * SSD_GROUP - j for j in range(SSD_GROUP)]
            step([fwd, bwd], with_out)
            return c

        lax.fori_loop(0, nc // SSD_GROUP, body, 0)

    s_ref[...] = jnp.zeros(s_ref.shape, F32)
    yacc_ref[...] = jnp.zeros(yacc_ref.shape, F32)
    phase(t_ctx, xc_ref, bc_ref, None, gc_ref, False)
    phase(t_lat, x_ref, b_ref, c_ref, g_ref, True)

    dsk = dsk_ref[...]
    nw = nw_ref[...]

    rb = ROW_BLOCK * LANE // XW

    def out_body(i, c):
        r0 = pl.multiple_of(i * rb, rb)
        y = yacc_ref[pl.ds(r0, rb), :] + dsk * xs_ref[pl.ds(r0, rb), :]
        y = y * _silu(z_ref[pl.ds(r0, rb), :])
        ms = jnp.mean(y * y, axis=-1, keepdims=True)
        o_ref[pl.ds(r0, rb), :] = (y * lax.rsqrt(ms + EPS) * nw).astype(o_ref.dtype)
        return c

    lax.fori_loop(0, t_lat // rb, out_body, 0)


def _ssd(u_lat, u_ctx, conv_w, conv_b, gpar, dskip, nw):
    bn, t, _ = u_lat.shape
    tc = u_ctx.shape[1]
    x_blk = lambda tt: pl.BlockSpec((None, tt, XW), lambda b, g: (b, 0, g))
    n_blk = lambda tt, off: pl.BlockSpec((None, tt, LANE), lambda b, g: (b, 0, off + g))
    b_off = SSD_INNER // LANE
    c_off = b_off + SSD_BC // LANE
    z_off = SSD_CONV_CH // XW
    dt_blk = (SSD_CONV_CH + SSD_INNER) // LANE
    fixed = lambda b, g: (b, 0, dt_blk)
    return pl.pallas_call(
        functools.partial(_ssd_kernel, t_lat=t, t_ctx=tc),
        grid=(bn, SSD_GROUPS),
        in_specs=[x_blk(t), n_blk(t, b_off), n_blk(t, c_off),
                  pl.BlockSpec((None, t, XW), lambda b, g: (b, 0, z_off + g)),
                  pl.BlockSpec((None, t, LANE), fixed),
                  x_blk(tc), n_blk(tc, b_off), pl.BlockSpec((None, tc, LANE), fixed),
                  pl.BlockSpec((CONV_W, XW), lambda b, g: (0, g)),
                  pl.BlockSpec((CONV_W, LANE), lambda b, g: (0, b_off + g)),
                  pl.BlockSpec((CONV_W, LANE), lambda b, g: (0, c_off + g)),
                  pl.BlockSpec((1, XW), lambda b, g: (0, g)),
                  pl.BlockSpec((1, LANE), lambda b, g: (0, b_off + g)),
                  pl.BlockSpec((1, LANE), lambda b, g: (0, c_off + g)),
                  pl.BlockSpec((SUBLANE, LANE), lambda b, g: (0, 0)),
                  pl.BlockSpec((None, 1, XW), lambda b, g: (g, 0, 0)),
                  pl.BlockSpec((None, 1, XW), lambda b, g: (g, 0, 0))],
        out_specs=pl.BlockSpec((None, t, XW), lambda b, g: (b, 0, g)),
        out_shape=jax.ShapeDtypeStruct((bn, t, SSD_INNER), BF16),
        scratch_shapes=[pltpu.VMEM((t + 2 * SUBLANE, XW), F32),
                        pltpu.VMEM((t + 2 * SUBLANE, LANE), F32),
                        pltpu.VMEM((t, XW), F32),
                        pltpu.VMEM((t, LANE), F32),
                        pltpu.VMEM((t, LANE), F32),
                        pltpu.VMEM((t, LANE), F32),
                        pltpu.VMEM((t, XW), F32),
                        pltpu.VMEM((2, SSD_STATE, XW), F32)],
        compiler_params=pltpu.CompilerParams(dimension_semantics=("arbitrary", "arbitrary"),
                                             vmem_limit_bytes=VMEM_LIMIT),
        name="ssd",
    )(u_lat, u_lat, u_lat, u_lat, u_lat, u_ctx, u_ctx, u_ctx,
      conv_w, conv_w, conv_w, conv_b, conv_b, conv_b, gpar, dskip, nw)


def _gdn_gate_params(a_log, dt_bias):
    z = jnp.zeros((SUBLANE, LANE), F32)
    z = z.at[0, 2 * GDN_HEADS:4 * GDN_HEADS].set(a_log.reshape(-1))
    return z.at[1, 2 * GDN_HEADS:4 * GDN_HEADS].set(dt_bias.reshape(-1))


def _ssd_gate_params(a_log, dt_bias):
    z = jnp.zeros((SUBLANE, LANE), F32)
    z = z.at[0, 0:2 * SSD_HEADS].set(a_log.reshape(-1))
    return z.at[1, 0:2 * SSD_HEADS].set(dt_bias.reshape(-1))


MIX_TM = 256
TOK_TILE = 128


def _mix_kernel(go_ref, so_ref, x_ref, g1_ref, sh2_ref, sc2_ref, npm_ref, npf_ref, wo_ref, wr_ref, br_ref,
                x1_ref, h2_ref, et_ref, gt_ref):
    so = so_ref[...]
    rows = MIX_TM // GRID_W
    ssd = jnp.concatenate([so[:, r * SSD_INNER:(r + 1) * SSD_INNER] for r in range(rows)], axis=0)
    m = _dot(go_ref[...], wo_ref[0:GDN_V, :]) + _dot(ssd, wo_ref[GDN_V:GDN_V + SSD_INNER, :])
    ms = jnp.mean(m * m, axis=-1, keepdims=True)
    x1 = x_ref[...] + g1_ref[...] * (m * lax.rsqrt(ms + EPS) * npm_ref[...])
    x1_ref[...] = x1
    h2 = _modulated_norm(x1, npf_ref[...], sh2_ref[...], sc2_ref[...])
    h2_ref[...] = h2
    logits = _dot_f32(h2, wr_ref[...]) + br_ref[...]
    lane = _iota(logits.shape, 1)
    lane_f = lane.astype(F32)
    l = jnp.where(lane < N_EXPERTS, logits, NEG)
    vals = []
    for _ in range(TOP_K):
        mk = jnp.max(l, axis=1, keepdims=True)
        idx = jnp.min(jnp.where(l == mk, lane_f, float(LANE)), axis=1, keepdims=True)
        l = jnp.where(lane_f == idx, NEG, l)
        vals.append(mk)
    ex = [jnp.exp(v - vals[0]) for v in vals]
    den = ex[0] + ex[1] + ex[2] + ex[3]
    gates = jnp.zeros(logits.shape, F32)
    for k in range(TOP_K):
        gates = jnp.where(lane == k, ex[k] / den, gates)
    gt_ref[...] = gates
    lt = logits.T[0:N_EXPERTS, :]
    row_f = _iota(lt.shape, 0).astype(F32)
    ids = []
    for _ in range(TOP_K):
        mk = jnp.max(lt, axis=0, keepdims=True)
        idx = jnp.min(jnp.where(lt == mk, row_f, float(N_EXPERTS)), axis=0, keepdims=True)
        lt = jnp.where(row_f == idx, NEG, lt)
        ids.append(idx.astype(I32))
    pad = jnp.full((SUBLANE - TOP_K, lt.shape[1]), -1, I32)
    et_ref[...] = jnp.concatenate(ids + [pad], axis=0)


def _mix(gdn_o, ssd_o, x, mod3, npm, npf, w_out_b16, w_router_pad, b_router_pad):
    bn, t, d = x.shape
    n = bn * t
    rows = MIX_TM // GRID_W
    nt = t // MIX_TM
    ssd_v = ssd_o.reshape(bn, GRID_W, (t // GRID_W) * SSD_INNER)
    modrow = lambda blk: pl.BlockSpec((None, 1, d), lambda b, i: (b, 0, blk))
    return pl.pallas_call(
        _mix_kernel,
        grid=(bn, nt),
        in_specs=[pl.BlockSpec((None, MIX_TM, GDN_V), lambda b, i: (b, i, 0)),
                  pl.BlockSpec((None, GRID_W, rows * SSD_INNER), lambda b, i: (b, 0, i)),
                  pl.BlockSpec((None, MIX_TM, d), lambda b, i: (b, i, 0)),
                  modrow(2), modrow(3), modrow(4),
                  pl.BlockSpec((1, d), lambda b, i: (0, 0)),
                  pl.BlockSpec((1, d), lambda b, i: (0, 0)),
                  pl.BlockSpec((GDN_V + SSD_INNER, d), lambda b, i: (0, 0)),
                  pl.BlockSpec((d, LANE), lambda b, i: (0, 0)),
                  pl.BlockSpec((1, LANE), lambda b, i: (0, 0))],
        out_specs=[pl.BlockSpec((None, MIX_TM, d), lambda b, i: (b, i, 0)),
                   pl.BlockSpec((None, MIX_TM, d), lambda b, i: (b, i, 0)),
                   pl.BlockSpec((SUBLANE, MIX_TM), lambda b, i: (0, b * nt + i)),
                   pl.BlockSpec((None, MIX_TM, LANE), lambda b, i: (b, i, 0))],
        out_shape=[jax.ShapeDtypeStruct((bn, t, d), F32),
                   jax.ShapeDtypeStruct((bn, t, d), F32),
                   jax.ShapeDtypeStruct((SUBLANE, n), I32),
                   jax.ShapeDtypeStruct((bn, t, LANE), F32)],
        compiler_params=pltpu.CompilerParams(dimension_semantics=("arbitrary", "arbitrary"),
                                             vmem_limit_bytes=VMEM_LIMIT),
        name="mix",
    )(gdn_o, ssd_v, x, mod3, mod3, mod3, npm, npf, w_out_b16, w_router_pad, b_router_pad)


def _rank_kernel(ids_ref, dest_ref, meta_ref, tril_ref, cnt_ref, rank_ref, *, n_blocks):
    nrow = ids_ref.shape[0]
    ids = ids_ref[...]
    tril_ref[...] = (_iota((nrow, nrow), 0) > _iota((nrow, nrow), 1)).astype(BF16)
    upper = (_iota((LANE, LANE), 0) < _iota((LANE, LANE), 1)).astype(BF16)
    ones = jnp.ones((LANE, LANE), BF16)
    rank_ref[...] = jnp.zeros(rank_ref.shape, F32)

    def count(e, c):
        hit = ids == e
        mask = hit.astype(F32).astype(BF16)
        before = _dot(mask, upper)
        rowsum = _dot(mask, ones)
        rows_before = _dot(tril_ref[...], rowsum.astype(BF16))
        rank_ref[...] += jnp.where(hit, before + rows_before, 0.0)
        cnt_ref[pl.ds(e, 1), :] = rows_before[nrow - 1:nrow, :] + rowsum[nrow - 1:nrow, :]
        return c

    lax.fori_loop(0, N_EXPERTS, count, 0)
    cnt = cnt_ref[...]
    padded = jnp.floor((cnt + (MOE_BLOCK - 1)) * (1.0 / MOE_BLOCK)) * MOE_BLOCK
    erow = _iota(cnt.shape, 0)
    pad_end = padded
    s = 1
    while s < N_EXPERTS:
        pad_end = pad_end + jnp.where(erow >= s, pltpu.roll(pad_end, s, 0), 0.0)
        s *= 2
    cnt_ref[...] = pad_end - padded
    jblk = (_iota((SUBLANE, LANE), 0) * LANE + _iota((SUBLANE, LANE), 1)).astype(F32) * MOE_BLOCK
    block_e = jnp.zeros((SUBLANE, LANE), F32)
    for e in range(N_EXPERTS):
        block_e = block_e + (pad_end[e:e + 1, :] <= jblk).astype(F32)
    block_e = jnp.minimum(block_e, N_EXPERTS - 1.0)
    n_used = jnp.broadcast_to(pad_end[N_EXPERTS - 1:N_EXPERTS, :] * (1.0 / MOE_BLOCK), (SUBLANE, LANE))
    meta_ref[...] = jnp.concatenate([block_e, n_used], axis=0).astype(I32)

    def place(e, c):
        rank_ref[...] += jnp.where(ids == e, cnt_ref[pl.ds(e, 1), :], 0.0)
        return c

    lax.fori_loop(0, N_EXPERTS, place, 0)
    dest_ref[...] = rank_ref[...].astype(I32)


def _rank(ids2, n_blocks):
    nrow = ids2.shape[0]
    return pl.pallas_call(
        functools.partial(_rank_kernel, n_blocks=n_blocks),
        in_specs=[pl.BlockSpec((nrow, LANE), lambda: (0, 0))],
        out_specs=[pl.BlockSpec((nrow, LANE), lambda: (0, 0)),
                   pl.BlockSpec((2 * SUBLANE, LANE), lambda: (0, 0))],
        out_shape=[jax.ShapeDtypeStruct((nrow, LANE), I32),
                   jax.ShapeDtypeStruct((2 * SUBLANE, LANE), I32)],
        scratch_shapes=[pltpu.VMEM((nrow, nrow), BF16),
                        pltpu.VMEM((N_EXPERTS, LANE), F32),
                        pltpu.VMEM((nrow, LANE), F32)],
        compiler_params=pltpu.CompilerParams(vmem_limit_bytes=VMEM_LIMIT),
        name="rank",
    )(ids2)


DISPATCH_TILES = 1


def _dispatch_kernel(dest_ref, h_ref, xg_in_ref, xg_ref, sem):
    del xg_in_ref
    groups = TOK_TILE // SUBLANE

    def copy(g8, r, d):
        return pltpu.make_async_copy(h_ref.at[g8, pl.ds(r, 1), :], xg_ref.at[pl.ds(d, 1), :], sem)

    def issue(i, c):
        j = i // groups
        t8 = i % groups
        for r in range(SUBLANE):
            for k in range(TOP_K):
                copy(i, r, dest_ref[j, k, t8 * SUBLANE + r]).start(priority=k % 2)
        return c

    lax.fori_loop(0, DISPATCH_TILES * groups, issue, 0)

    def drain(i, c):
        for _ in range(SUBLANE * TOP_K):
            copy(0, 0, 0).wait()
        return c

    lax.fori_loop(0, DISPATCH_TILES * groups, drain, 0)


def _dispatch(dest3, h2, xg0):
    ntile = dest3.shape[0]
    d = h2.shape[1]
    h3 = h2.reshape(h2.shape[0] // SUBLANE, SUBLANE, d)
    groups = TOK_TILE // SUBLANE
    return pl.pallas_call(
        _dispatch_kernel,
        grid=(ntile // DISPATCH_TILES,),
        in_specs=[pl.BlockSpec((DISPATCH_TILES, TOP_K, TOK_TILE), lambda i: (i, 0, 0), memory_space=pltpu.SMEM),
                  pl.BlockSpec((DISPATCH_TILES * groups, SUBLANE, d), lambda i: (i, 0, 0)),
                  pl.BlockSpec(memory_space=pl.ANY)],
        out_specs=pl.BlockSpec(memory_space=pl.ANY),
        out_shape=jax.ShapeDtypeStruct(xg0.shape, xg0.dtype),
        scratch_shapes=[pltpu.SemaphoreType.DMA(())],
        input_output_aliases={2: 0},
        compiler_params=pltpu.CompilerParams(dimension_semantics=("arbitrary",)),
        name="dispatch",
    )(dest3, h3, xg0)


def _experts_kernel(be_ref, nu_ref, x_ref, wg_ref, wu_ref, wd_ref, bg_ref, bu_ref, bd_ref, y_ref):
    j = pl.program_id(0)

    @pl.when(j < nu_ref[0])
    def _():
        x = x_ref[...].astype(BF16)
        a = jnp.minimum(_dot(x, wg_ref[...]) + bg_ref[...], SWIGLU_LIMIT)
        b = jnp.clip(_dot(x, wu_ref[...]) + bu_ref[...], -SWIGLU_LIMIT, SWIGLU_LIMIT)
        hid = a * _sigmoid(SWIGLU_ALPHA * a) * (b + 1.0)
        y_ref[...] = _dot(hid.astype(BF16), wd_ref[...]) + bd_ref[...]

    @pl.when(j >= nu_ref[0])
    def _():
        y_ref[...] = jnp.zeros(y_ref.shape, y_ref.dtype)


def _experts(block_e, n_used, xg, wg, wu, wd, bg, bu, bd):
    cap, d = xg.shape
    n_blocks = cap // MOE_BLOCK
    de = wg.shape[2]
    wspec = lambda a, b: pl.BlockSpec((None, a, b), lambda j, be, nu: (be[j], 0, 0))
    grid_spec = pltpu.PrefetchScalarGridSpec(
        num_scalar_prefetch=2,
        grid=(n_blocks,),
        in_specs=[pl.BlockSpec((MOE_BLOCK, d), lambda j, be, nu: (jnp.minimum(j, nu[0] - 1), 0)),
                  wspec(d, de), wspec(d, de), wspec(de, d),
                  wspec(1, de), wspec(1, de), wspec(1, d)],
        out_specs=pl.BlockSpec((MOE_BLOCK, d), lambda j, be, nu: (j, 0)),
    )
    return pl.pallas_call(
        _experts_kernel,
        grid_spec=grid_spec,
        out_shape=jax.ShapeDtypeStruct((cap, d), F32),
        compiler_params=pltpu.CompilerParams(dimension_semantics=("arbitrary",),
                                             vmem_limit_bytes=VMEM_LIMIT),
        name="experts",
    )(block_e, n_used, xg, wg, wu, wd, bg, bu, bd)


def _combine_kernel(dcur_ref, dnext_ref, yg_ref, gt_ref, x1_ref, g2_ref, nw_ref, o_ref, ybuf, sems):
    i = pl.program_id(0)
    last = pl.num_programs(0) - 1
    groups = TOK_TILE // SUBLANE

    def copy(slot, k, t8, r, d):
        return pltpu.make_async_copy(yg_ref.at[pl.ds(d, 1), :], ybuf.at[slot, k, t8, pl.ds(r, 1), :],
                                     sems.at[slot])

    def issue(dref, slot):
        def body(t8, c):
            for r in range(SUBLANE):
                for k in range(TOP_K):
                    copy(slot, k, t8, r, dref[k, t8 * SUBLANE + r]).start(priority=k % 2)
            return c

        lax.fori_loop(0, groups, body, 0)

    @pl.when(i == 0)
    def _():
        issue(dcur_ref, 0)

    @pl.when(i < last)
    def _():
        issue(dnext_ref, (i + 1) % 2)

    slot = i % 2

    def drain(t8, c):
        for _ in range(SUBLANE * TOP_K):
            copy(slot, 0, 0, 0, 0).wait()
        return c

    lax.fori_loop(0, groups, drain, 0)
    d_model = o_ref.shape[-1]
    gt = gt_ref[...]
    f = gt[:, 0:1] * ybuf[slot, 0].reshape(TOK_TILE, d_model)
    for k in range(1, TOP_K):
        f = f + gt[:, k:k + 1] * ybuf[slot, k].reshape(TOK_TILE, d_model)
    ms = jnp.mean(f * f, axis=-1, keepdims=True)
    o_ref[...] = x1_ref[...] + g2_ref[...] * (f * lax.rsqrt(ms + EPS) * nw_ref[...])


def _combine(dest3, yg, gates_tok, x1, mod3, npf2, t_per_batch):
    n, d = x1.shape
    ntile = n // TOK_TILE
    per_b = t_per_batch // TOK_TILE
    return pl.pallas_call(
        _combine_kernel,
        grid=(ntile,),
        in_specs=[pl.BlockSpec((None, TOP_K, TOK_TILE), lambda i: (i, 0, 0), memory_space=pltpu.SMEM),
                  pl.BlockSpec((None, TOP_K, TOK_TILE), lambda i: (jnp.minimum(i + 1, ntile - 1), 0, 0),
                               memory_space=pltpu.SMEM),
                  pl.BlockSpec(memory_space=pl.ANY),
                  pl.BlockSpec((TOK_TILE, LANE), lambda i: (i, 0)),
                  pl.BlockSpec((TOK_TILE, d), lambda i: (i, 0)),
                  pl.BlockSpec((None, 1, d), lambda i: (i // per_b, 0, 5)),
                  pl.BlockSpec((1, d), lambda i: (0, 0))],
        out_specs=pl.BlockSpec((TOK_TILE, d), lambda i: (i, 0)),
        out_shape=jax.ShapeDtypeStruct((n, d), F32),
        scratch_shapes=[pltpu.VMEM((2, TOP_K, TOK_TILE // SUBLANE, SUBLANE, d), F32),
                        pltpu.SemaphoreType.DMA((2,))],
        compiler_params=pltpu.CompilerParams(dimension_semantics=("arbitrary",),
                                             vmem_limit_bytes=VMEM_LIMIT),
        name="combine",
    )(dest3, dest3, yg, gates_tok, x1, mod3, npf2)


def _pad_cols(w, cols):
    return jnp.pad(w, ((0, 0), (0, cols - w.shape[1])))


def kernel(x, c, ctx, c_ctx, w_ada, b_ada, norm_pre_mix, norm_post_mix, norm_pre_ffn, norm_post_ffn,
           w_in, gdn_conv_w, gdn_A_log, gdn_dt_bias, gdn_norm_w, ssd_conv_w, ssd_conv_b, ssd_A_log,
           ssd_dt_bias, ssd_D, ssd_norm_w, w_out, w_router, b_router, w_gate, b_gate, w_up, b_up,
           w_down, b_down):
    bn, t, d = x.shape
    n = bn * t
    l = 0
    mod_rows = -(-(bn + 1) // SUBLANE) * SUBLANE
    cc = jnp.zeros((mod_rows, d), F32).at[:bn].set(c).at[bn].set(c_ctx)
    mod3 = _ada(cc, w_ada[l], b_ada[l]).reshape(mod_rows, 1, 6 * d)

    w_in_b = w_in[l].astype(BF16)
    w_gdn = _pad_cols(w_in_b[:, :GDN_COLS], GDN_COLS_PAD)
    w_ssd = _pad_cols(w_in_b[:, GDN_COLS:], SSD_COLS_PAD)
    npm1 = norm_pre_mix[l].reshape(1, d)
    ug = _inproj(x, mod3, npm1, w_gdn, mod_row=None, col_major=False)
    us = _inproj(x, mod3, npm1, w_ssd, mod_row=None, col_major=True)
    ugc = _inproj(ctx, mod3, npm1, w_gdn, mod_row=bn, col_major=False)
    usc = _inproj(ctx, mod3, npm1, w_ssd, mod_row=bn, col_major=False)

    gdn_o = _gdn(ug, ugc, gdn_conv_w[l], _gdn_gate_params(gdn_A_log[l], gdn_dt_bias[l]),
                 jnp.tile(gdn_norm_w[l], 2).reshape(1, LANE))
    ssd_o = _ssd(us, usc, ssd_conv_w[l], ssd_conv_b[l].reshape(1, SSD_CONV_CH),
                 _ssd_gate_params(ssd_A_log[l], ssd_dt_bias[l]),
                 jnp.repeat(ssd_D[l], SSD_HEADDIM).reshape(SSD_GROUPS, 1, XW),
                 ssd_norm_w[l].reshape(SSD_GROUPS, 1, XW))

    x1, h2, e_t, gates_tok = _mix(
        gdn_o, ssd_o, x, mod3, norm_post_mix[l].reshape(1, d), norm_pre_ffn[l].reshape(1, d),
        w_out[l].astype(BF16), _pad_cols(w_router[l], LANE), _pad_cols(b_router[l].reshape(1, N_EXPERTS), LANE))

    n_blocks = -(-(n * TOP_K) // MOE_BLOCK) + N_EXPERTS
    cap = n_blocks * MOE_BLOCK
    ntile = n // TOK_TILE
    ids2 = e_t.reshape(SUBLANE * ntile, LANE)[:TOP_K * ntile]
    dest, meta = _rank(ids2, n_blocks)
    dest3 = dest.reshape(TOP_K, ntile, LANE).transpose(1, 0, 2)
    block_e = meta[:SUBLANE].reshape(-1)[:n_blocks]
    n_used = meta[SUBLANE, :1]

    xg = _dispatch(dest3, h2.reshape(n, d), jnp.zeros((cap, d), F32))
    yg = _experts(block_e, n_used, xg,
                  w_gate[l].astype(BF16), w_up[l].astype(BF16), w_down[l].astype(BF16),
                  b_gate[l].reshape(N_EXPERTS, 1, D_EXPERT), b_up[l].reshape(N_EXPERTS, 1, D_EXPERT),
                  b_down[l].reshape(N_EXPERTS, 1, d))
    out = _combine(dest3, yg, gates_tok.reshape(n, LANE), x1.reshape(n, d), mod3,
                   norm_post_ffn[l].reshape(1, d), t)
    return out.reshape(bn, t, d)
```

```python
import functools

import jax
import jax.numpy as jnp
from jax import lax
from jax.experimental import pallas as pl
from jax.experimental.pallas import tpu as pltpu

F32 = jnp.float32
BF16 = jnp.bfloat16
I32 = jnp.int32

D_MODEL = 1024
GRID_W = 64
GDN_HEADS = 8
GDN_DK = 64
GDN_DV = 64
SSD_HEADS = 8
SSD_HEADDIM = 64
SSD_GROUPS = 2
SSD_STATE = 128
CONV_W = 5
CHUNK = 64
N_EXPERTS = 32
TOP_K = 4
D_EXPERT = 1024
SWIGLU_LIMIT = 7.0
SWIGLU_ALPHA = 1.702
MOE_BLOCK = 256
EPS = 1e-6

GDN_QK = GDN_HEADS * GDN_DK
GDN_V = GDN_HEADS * GDN_DV
GDN_CONV_CH = 2 * GDN_QK + GDN_V
GDN_COLS = GDN_CONV_CH + GDN_V + 4 * GDN_HEADS
SSD_INNER = SSD_HEADS * SSD_HEADDIM
SSD_BC = SSD_GROUPS * SSD_STATE
SSD_CONV_CH = SSD_INNER + 2 * SSD_BC
SSD_COLS = SSD_CONV_CH + SSD_INNER + 2 * SSD_HEADS

LANE = 128
SUBLANE = 8
GDN_COLS_PAD = 17 * LANE
SSD_COLS_PAD = 13 * LANE
VMEM_LIMIT = 56 * 1024 * 1024
NEG = -1e30
ROW_BLOCK = 256
INTRA_GROUP = 8
SSD_GROUP = 2


def _dot(a, b):
    return jnp.dot(a, b, preferred_element_type=F32)


def _dot_nt(a, b):
    return lax.dot_general(a, b, (((1,), (1,)), ((), ())), preferred_element_type=F32)


def _dot_f32(a, b):
    return jnp.dot(a, b, preferred_element_type=F32, precision=lax.Precision.HIGHEST)


def _silu(x):
    return x * (1.0 / (1.0 + jnp.exp(-x)))


def _sigmoid(x):
    return 1.0 / (1.0 + jnp.exp(-x))


def _softplus(x):
    return jnp.maximum(x, 0.0) + jnp.log(1.0 + jnp.exp(-jnp.abs(x)))


def _iota(shape, axis):
    return lax.broadcasted_iota(I32, shape, axis)


def _split_dot(x, ones_b16):
    hi = x.astype(BF16)
    lo = (x - hi.astype(F32)).astype(BF16)
    return _dot(hi, ones_b16) + _dot(lo, ones_b16)


def _ada_kernel(c_ref, w_ref, b_ref, o_ref):
    o_ref[...] = _dot_f32(_silu(c_ref[...]), w_ref[...]) + b_ref[...]


def _ada(cc, w_ada, b_ada):
    rows, d = cc.shape
    n = w_ada.shape[1]
    tn = 1024
    return pl.pallas_call(
        _ada_kernel,
        grid=(n // tn,),
        in_specs=[pl.BlockSpec((rows, d), lambda j: (0, 0)),
                  pl.BlockSpec((d, tn), lambda j: (0, j)),
                  pl.BlockSpec((1, tn), lambda j: (0, j))],
        out_specs=pl.BlockSpec((rows, tn), lambda j: (0, j)),
        out_shape=jax.ShapeDtypeStruct((rows, n), F32),
        compiler_params=pltpu.CompilerParams(dimension_semantics=("arbitrary",),
                                             vmem_limit_bytes=VMEM_LIMIT),
        name="ada",
    )(cc, w_ada, b_ada.reshape(1, n))


def _modulated_norm(x, nw, sh, sc):
    ms = jnp.mean(x * x, axis=-1, keepdims=True)
    return (x * lax.rsqrt(ms + EPS) * nw) * (1.0 + sc) + sh


def _inproj_kernel(x_ref, sh_ref, sc_ref, nw_ref, w_ref, o_ref, *, pieces):
    if pieces:
        xt = x_ref[...]
        x = jnp.concatenate([xt[:, i * D_MODEL:(i + 1) * D_MODEL] for i in range(pieces)], axis=0)
    else:
        x = x_ref[...]
    h = _modulated_norm(x, nw_ref[...], sh_ref[...], sc_ref[...])
    o_ref[...] = _dot(h.astype(BF16), w_ref[...])


def _inproj(x, mod3, nw, w_b16, *, mod_row, col_major):
    bn, t, d = x.shape
    cols = w_b16.shape[1]
    if col_major:
        rows = t // GRID_W
        pieces = 8
        tm = pieces * rows
        xin = x.reshape(bn, rows, GRID_W * d)
        x_spec = pl.BlockSpec((None, rows, pieces * d), lambda b, i: (b, 0, i))
    else:
        pieces = 0
        tm = min(512, t)
        xin = x
        x_spec = pl.BlockSpec((None, tm, d), lambda b, i: (b, i, 0))
    if mod_row is None:
        row = lambda b: b
    else:
        row = lambda b: mod_row
    return pl.pallas_call(
        functools.partial(_inproj_kernel, pieces=pieces),
        grid=(bn, t // tm),
        in_specs=[x_spec,
                  pl.BlockSpec((None, 1, d), lambda b, i: (row(b), 0, 0)),
                  pl.BlockSpec((None, 1, d), lambda b, i: (row(b), 0, 1)),
                  pl.BlockSpec((1, d), lambda b, i: (0, 0)),
                  pl.BlockSpec((d, cols), lambda b, i: (0, 0))],
        out_specs=pl.BlockSpec((None, tm, cols), lambda b, i: (b, i, 0)),
        out_shape=jax.ShapeDtypeStruct((bn, t, cols), F32),
        compiler_params=pltpu.CompilerParams(dimension_semantics=("arbitrary", "arbitrary"),
                                             vmem_limit_bytes=VMEM_LIMIT),
        name="inproj_cm" if col_major else "inproj",
    )(xin, mod3, mod3, nw, w_b16)


def _conv_silu(src_ref, t, pad_ref, w_ref, bias, dst_ref, post=None):
    width = src_ref.shape[-1]
    zero = jnp.zeros((SUBLANE, width), F32)
    pad_ref[0:SUBLANE, :] = zero
    pad_ref[pl.ds(SUBLANE + t, SUBLANE), :] = zero
    rb = ROW_BLOCK * LANE // width
    nb = t // rb

    def copy(i, c):
        r0 = pl.multiple_of(i * rb, rb)
        pad_ref[pl.ds(SUBLANE + r0, rb), :] = src_ref[pl.ds(r0, rb), :]
        return c

    lax.fori_loop(0, nb, copy, 0)
    w = w_ref[...]

    def body(i, c):
        r0 = pl.multiple_of(i * rb, rb)
        win = pad_ref[pl.ds(r0, rb + 2 * SUBLANE), :]
        acc = None
        for j in range(CONV_W):
            off = SUBLANE - CONV_W // 2 + j
            tap = pltpu.roll(win, rb + 2 * SUBLANE - off, 0)[0:rb, :] * w[j:j + 1, :]
            acc = tap if acc is None else acc + tap
        if bias is not None:
            acc = acc + bias
        y = _silu(acc)
        if post is not None:
            y = post(y)
        dst_ref[pl.ds(r0, rb), :] = y
        return c

    lax.fori_loop(0, nb, body, 0)


def _chunk_cumsums(x, rows_in_chunk):
    n = x.shape[0]
    pre = x
    suf = x
    s = 1
    while s < CHUNK:
        pre = pre + jnp.where(rows_in_chunk >= s, pltpu.roll(pre, s, 0), 0.0)
        suf = suf + jnp.where(rows_in_chunk < CHUNK - s, pltpu.roll(suf, n - s, 0), 0.0)
        s *= 2
    return pre, suf


def _stack_masked(x, groups):
    r, n = x.shape
    w = n // groups
    lane_group = _iota((r, n), 1) // w
    return jnp.concatenate([jnp.where(lane_group == g, x, 0.0) for g in range(groups)], axis=0)


def _gdn_kernel(q_ref, k_ref, v_ref, z_ref, g_ref, kc_ref, vc_ref, gc_ref,
                wq_ref, wk_ref, wv_ref, gpar_ref, nw_ref, o_ref,
                pad_ref, qn_ref, kn_ref, vv_ref, gb_ref, u_ref, w_ref, at_ref, qe_ref,
                m_ref, c_ref, ss_ref, egl_ref, oacc_ref, s_ref, *, t_lat, t_ctx):
    hp = pl.program_id(1)
    ones_bd = (_iota((LANE, LANE), 0) // GDN_DK == _iota((LANE, LANE), 1) // GDN_DK).astype(BF16)
    lane64 = _iota((CHUNK, LANE), 1)
    first = lane64 < GDN_DK
    ii = _iota((CHUNK, LANE), 0)
    jj = lane64 % CHUNK
    eye = (ii == jj).astype(F32)
    bd_mask = (_iota((LANE, LANE), 0) // GDN_DK) == (_iota((LANE, LANE), 1) // GDN_DV)
    gpar = gpar_ref[...]
    a_row = jnp.exp(gpar[0:1, :])
    dtb_row = gpar[1:2, :]

    def l2norm(scale):
        def f(y):
            ss = _split_dot(y * y, ones_bd)
            return y * lax.rsqrt(ss + EPS) * scale
        return f

    def gate_block(src_ref, t):
        lane = _iota((ROW_BLOCK, LANE), 1)
        ric = _iota((ROW_BLOCK, LANE), 0) % CHUNK
        shift = (LANE - 2 * hp) % LANE

        def body(i, c):
            r0 = pl.multiple_of(i * ROW_BLOCK, ROW_BLOCK)
            x = src_ref[pl.ds(r0, ROW_BLOCK), :]
            act = jnp.where(lane < 2 * GDN_HEADS, _sigmoid(x), -a_row * _softplus(x + dtb_row))
            pre, suf = _chunk_cumsums(act, ric)
            blk = jnp.where(lane < 2 * GDN_HEADS, act, jnp.where(lane < 3 * GDN_HEADS, pre, suf))
            gb_ref[pl.ds(r0, ROW_BLOCK), :] = pltpu.roll(blk, shift, 1)
            return c

        lax.fori_loop(0, t // ROW_BLOCK, body, 0)

    def expand(tile, c0):
        return jnp.where(first, tile[:, c0:c0 + 1], tile[:, c0 + 1:c0 + 2])

    first2 = jnp.concatenate([first, first], axis=1)
    dirs = (0, 1)

    def intra(ns, with_out):
        r0s = [pl.multiple_of(n * CHUNK, CHUNK) for n in ns]
        ks = [kn_ref[pl.ds(r0, CHUNK), :] for r0 in r0s]
        gs = [gb_ref[pl.ds(r0, CHUNK), :] for r0 in r0s]
        gts = [jnp.concatenate([g, g], axis=0).T for g in gs]
        k_bds = [_stack_masked(k, 2).astype(BF16) for k in ks]
        probs = [(c, d) for c in range(len(ns)) for d in dirs]
        beta, gx, rel, gl, kb = {}, {}, {}, {}, {}
        for c, d in probs:
            g = gs[c]
            c0 = 2 * GDN_HEADS + GDN_HEADS * d
            beta[c, d] = expand(g, GDN_HEADS * d)
            gx[c, d] = expand(g, c0)
            grow = jnp.where(first[0:1, :], gts[c][c0:c0 + 1, :], gts[c][c0 + 1:c0 + 2, :])
            rel[c, d] = gx[c, d] - grow
            gl[c, d] = gx[c, d][CHUNK - 1:CHUNK, :] if d == 0 else gx[c, d][0:1, :]
            kb[c, d] = ks[c] * beta[c, d]
        strict = {0: ii > jj, 1: ii < jj}
        incl = {0: ii >= jj, 1: ii <= jj}
        lhs = []
        for c in range(len(ns)):
            parts = [kb[c, 0], kb[c, 1]]
            if with_out:
                parts.append(qn_ref[pl.ds(r0s[c], CHUNK), :])
            lhs.append(jnp.concatenate(parts, axis=0).astype(BF16))
        kk = [_dot_nt(lhs[c], k_bds[c]) for c in range(len(ns))]
        nm, p = {}, {}
        for c, d in probs:
            a = kk[c][d * CHUNK:(d + 1) * CHUNK, :] * jnp.exp(jnp.where(strict[d], rel[c, d], NEG))
            nm[c, d] = -a
            p[c, d] = eye + nm[c, d]
        for key in probs:
            nm[key] = _dot(nm[key].astype(BF16), _stack_masked(nm[key], 2).astype(BF16))
        for _ in range(4):
            for key in probs:
                both = _dot(jnp.concatenate([nm[key], p[key]], axis=0).astype(BF16),
                            _stack_masked(nm[key], 2).astype(BF16))
                nm[key] = both[0:CHUNK, :]
                p[key] = p[key] + both[CHUNK:2 * CHUNK, :]
        for key in probs:
            p[key] = p[key] + _dot(p[key].astype(BF16), _stack_masked(nm[key], 2).astype(BF16))
        vs = [vv_ref[pl.ds(r0, CHUNK), :] for r0 in r0s]
        eg, uw = {}, {}
        for c, d in probs:
            eg[c, d] = jnp.exp(gx[c, d])
            rhs = jnp.concatenate([vs[c] * beta[c, d], kb[c, d] * eg[c, d]], axis=1)
            rhs_bd = jnp.concatenate([jnp.where(first2, rhs, 0.0), jnp.where(first2, 0.0, rhs)], axis=0)
            uw[c, d] = _dot(p[c, d].astype(BF16), rhs_bd.astype(BF16))
        mc = {}
        for c, d in probs:
            kdec = ks[c] * jnp.exp(gl[c, d] - gx[c, d])
            kdt = jnp.concatenate([kdec, jnp.zeros_like(kdec)], axis=0).T
            wu = jnp.concatenate([uw[c, d][:, LANE:], uw[c, d][:, :LANE]], axis=1).astype(BF16)
            mc[c, d] = _dot(kdt.astype(BF16), jnp.concatenate([wu, jnp.zeros_like(wu)], axis=0))
        for c, d in probs:
            r0, n = r0s[c], ns[c]
            m0 = pl.multiple_of(n * LANE, LANE)
            m_ref[d, pl.ds(m0, LANE), :] = jnp.where(bd_mask, mc[c, d][:, :LANE], 0.0).astype(BF16)
            c_ref[d, pl.ds(m0, LANE), :] = jnp.where(bd_mask, mc[c, d][:, LANE:], 0.0)
            egl_ref[d, n] = jnp.broadcast_to(jnp.exp(gl[c, d]), (SUBLANE, LANE))
            if with_out:
                u_ref[d, pl.ds(r0, CHUNK), :] = uw[c, d][:, :LANE]
                w_ref[d, pl.ds(r0, CHUNK), :] = uw[c, d][:, LANE:].astype(BF16)
                at = kk[c][2 * CHUNK:3 * CHUNK, :] * jnp.exp(jnp.where(incl[d], rel[c, d], NEG))
                at_ref[d, pl.ds(r0, CHUNK), :] = at.astype(BF16)
                qe_ref[d, pl.ds(r0, CHUNK), :] = (qn_ref[pl.ds(r0, CHUNK), :] * eg[c, d]).astype(BF16)

    def scan_step(ns, with_out):
        m0s = [pl.multiple_of(n * LANE, LANE) for n in ns]
        s = [s_ref[d] for d in dirs]
        sb = [s[d].astype(BF16) for d in dirs]
        ms = [_dot(m_ref[d, pl.ds(m0s[d], LANE), :], sb[d]) for d in dirs]
        for d in dirs:
            s_ref[d] = s[d] * egl_ref[d, ns[d]][0:1, :] - ms[d] + c_ref[d, pl.ds(m0s[d], LANE), :]
            if with_out:
                ss_ref[d, pl.ds(m0s[d], LANE), :] = sb[d]

    def outputs(ns):
        cs = range(len(ns))
        probs = [(c, d) for c in cs for d in dirs]
        r0 = [pl.multiple_of(n * CHUNK, CHUNK) for n in ns]
        m0 = [pl.multiple_of(n * LANE, LANE) for n in ns]
        ws = {(c, d): _dot(jnp.concatenate([w_ref[d, pl.ds(r0[c], CHUNK), :], qe_ref[d, pl.ds(r0[c], CHUNK), :]],
                                           axis=0), ss_ref[d, pl.ds(m0[c], LANE), :]) for c, d in probs}
        vnew = {(c, d): u_ref[d, pl.ds(r0[c], CHUNK), :] - ws[c, d][0:CHUNK, :] for c, d in probs}
        av = {(c, d): _dot(at_ref[d, pl.ds(r0[c], CHUNK), :], _stack_masked(vnew[c, d], 2).astype(BF16))
              for c, d in probs}
        for c in cs:
            oacc_ref[pl.ds(r0[c], CHUNK), :] = (ws[c, 0][CHUNK:2 * CHUNK, :] + av[c, 0]
                                                + ws[c, 1][CHUNK:2 * CHUNK, :] + av[c, 1])

    def phase(t, k_src, v_src, g_src, q_src, with_out):
        nc = t // CHUNK
        _conv_silu(k_src, t, pad_ref, wk_ref, None, kn_ref, l2norm(1.0))
        _conv_silu(v_src, t, pad_ref, wv_ref, None, vv_ref)
        if with_out:
            _conv_silu(q_src, t, pad_ref, wq_ref, None, qn_ref, l2norm(GDN_DK ** -0.5))
        gate_block(g_src, t)

        group = min(INTRA_GROUP, nc)

        def intra_body(i, c):
            intra([i * group + j for j in range(group)], with_out)
            return c

        lax.fori_loop(0, nc // group, intra_body, 0)

        def scan_body(i, c):
            scan_step([i, nc - 1 - i], with_out)
            return c

        lax.fori_loop(0, nc, scan_body, 0)
        if with_out:
            def out_group(i, c):
                outputs([i * group + j for j in range(group)])
                return c

            lax.fori_loop(0, nc // group, out_group, 0)

    s_ref[...] = jnp.zeros(s_ref.shape, F32)
    phase(t_ctx, kc_ref, vc_ref, gc_ref, None, False)
    phase(t_lat, k_ref, v_ref, g_ref, q_ref, True)

    nw = nw_ref[...]

    def out_body(i, c):
        r0 = pl.multiple_of(i * ROW_BLOCK, ROW_BLOCK)
        o = oacc_ref[pl.ds(r0, ROW_BLOCK), :]
        ms = _split_dot(o * o, ones_bd) * (1.0 / GDN_DV)
        y = o * lax.rsqrt(ms + EPS) * nw * _silu(z_ref[pl.ds(r0, ROW_BLOCK), :])
        o_ref[pl.ds(r0, ROW_BLOCK), :] = y.astype(o_ref.dtype)
        return c

    lax.fori_loop(0, t_lat // ROW_BLOCK, out_body, 0)


def _gdn(u_lat, u_ctx, conv_w, gpar, nw2):
    bn, t, _ = u_lat.shape
    tc = u_ctx.shape[1]
    npairs = GDN_HEADS // 2
    nc = t // CHUNK
    col = lambda off: (lambda b, h: (b, 0, off + h))
    lat = lambda off: pl.BlockSpec((None, t, LANE), col(off))
    ctx = lambda off: pl.BlockSpec((None, tc, LANE), col(off))
    gate_blk = GDN_CONV_CH // LANE + GDN_V // LANE
    fixed = lambda b, h: (b, 0, gate_blk)
    cw = lambda off: pl.BlockSpec((CONV_W, LANE), lambda b, h: (0, off + h))
    return pl.pallas_call(
        functools.partial(_gdn_kernel, t_lat=t, t_ctx=tc),
        grid=(bn, npairs),
        in_specs=[lat(0), lat(npairs), lat(2 * npairs), lat(3 * npairs),
                  pl.BlockSpec((None, t, LANE), fixed),
                  ctx(npairs), ctx(2 * npairs), pl.BlockSpec((None, tc, LANE), fixed),
                  cw(0), cw(npairs), cw(2 * npairs),
                  pl.BlockSpec((SUBLANE, LANE), lambda b, h: (0, 0)),
                  pl.BlockSpec((1, LANE), lambda b, h: (0, 0))],
        out_specs=pl.BlockSpec((None, t, LANE), lambda b, h: (b, 0, h)),
        out_shape=jax.ShapeDtypeStruct((bn, t, GDN_V), BF16),
        scratch_shapes=[pltpu.VMEM((t + 2 * SUBLANE, LANE), F32),
                        pltpu.VMEM((t, LANE), F32),
                        pltpu.VMEM((t, LANE), F32),
                        pltpu.VMEM((t, LANE), F32),
                        pltpu.VMEM((t, LANE), F32),
                        pltpu.VMEM((2, t, LANE), F32),
                        pltpu.VMEM((2, t, LANE), BF16),
                        pltpu.VMEM((2, t, LANE), BF16),
                        pltpu.VMEM((2, t, LANE), BF16),
                        pltpu.VMEM((2, nc * LANE, LANE), BF16),
                        pltpu.VMEM((2, nc * LANE, LANE), F32),
                        pltpu.VMEM((2, nc * LANE, LANE), BF16),
                        pltpu.VMEM((2, nc, SUBLANE, LANE), F32),
                        pltpu.VMEM((t, LANE), F32),
                        pltpu.VMEM((2, LANE, LANE), F32)],
        compiler_params=pltpu.CompilerParams(dimension_semantics=("arbitrary", "arbitrary"),
                                             vmem_limit_bytes=VMEM_LIMIT),
        name="gdn",
    )(u_lat, u_lat, u_lat, u_lat, u_lat, u_ctx, u_ctx, u_ctx, conv_w, conv_w, conv_w, gpar, nw2)


HG = SSD_HEADS // SSD_GROUPS
XW = HG * SSD_HEADDIM


def _ssd_kernel(x_ref, b_ref, c_ref, z_ref, g_ref, xc_ref, bc_ref, gc_ref,
                wx_ref, wb_ref, wc_ref, bx_ref, bb_ref, bcb_ref, gpar_ref, dsk_ref, nw_ref, o_ref,
                padx_ref, padb_ref, xs_ref, bs_ref, cs_ref, gb_ref, yacc_ref, s_ref, *, t_lat, t_ctx):
    grp = pl.program_id(1)
    lane4 = _iota((CHUNK, XW), 1)
    lg = lane4 // SSD_HEADDIM
    ii = _iota((CHUNK, XW), 0)
    jj = lane4 % CHUNK
    gpar = gpar_ref[...]
    a_row = -jnp.exp(gpar[0:1, :])
    dtb_row = gpar[1:2, :]

    def gate_block(src_ref, t):
        lane = _iota((ROW_BLOCK, LANE), 1)
        ric = _iota((ROW_BLOCK, LANE), 0) % CHUNK
        shift = (LANE - HG * grp) % LANE

        def body(i, c):
            r0 = pl.multiple_of(i * ROW_BLOCK, ROW_BLOCK)
            x = src_ref[pl.ds(r0, ROW_BLOCK), :]
            dt = _softplus(x + dtb_row)
            a = pltpu.roll(dt * a_row, 2 * SSD_HEADS, 1)
            pre, suf = _chunk_cumsums(a, ric)
            blk = jnp.where(lane < 2 * SSD_HEADS, dt, jnp.where(lane < 3 * SSD_HEADS, pre, suf))
            gb_ref[pl.ds(r0, ROW_BLOCK), :] = pltpu.roll(blk, shift, 1)
            return c

        lax.fori_loop(0, t // ROW_BLOCK, body, 0)

    def expand(tile, c0):
        out = tile[:, c0 + HG - 1:c0 + HG]
        for h in range(HG - 2, -1, -1):
            out = jnp.where(lg == h, tile[:, c0 + h:c0 + h + 1], out)
        return out

    def rowsel(gt2, c0):
        out = gt2[c0 + HG - 1:c0 + HG, :]
        for h in range(HG - 2, -1, -1):
            out = jnp.where(lg[0:1, :] == h, gt2[c0 + h:c0 + h + 1, :], out)
        return out

    def step(chunks, with_out):
        probs = [(d, j) for d in (0, 1) for j in range(len(chunks[d]))]
        r0, ax, al, x, bm, cm, xw, upd, yin = {}, {}, {}, {}, {}, {}, {}, {}, {}
        for key in probs:
            d, j = key
            r0[key] = pl.multiple_of(chunks[d][j] * CHUNK, CHUNK)
            g = gb_ref[pl.ds(r0[key], CHUNK), :]
            dtx = expand(g, SSD_HEADS * d)
            ax[key] = expand(g, 2 * SSD_HEADS + SSD_HEADS * d)
            al[key] = ax[key][CHUNK - 1:CHUNK, :] if d == 0 else ax[key][0:1, :]
            x[key] = xs_ref[pl.ds(r0[key], CHUNK), :]
            bm[key] = bs_ref[pl.ds(r0[key], CHUNK), :]
            xw[key] = (x[key] * (jnp.exp(al[key] - ax[key]) * dtx)).astype(BF16)
            if with_out:
                gt = jnp.concatenate([g, g], axis=0).T
                gt2 = jnp.concatenate([gt, gt], axis=1)
                arow = rowsel(gt2, 2 * SSD_HEADS + SSD_HEADS * d)
                dtrow = rowsel(gt2, SSD_HEADS * d)
                incl = (ii >= jj) if d == 0 else (ii <= jj)
                yin[key] = jnp.exp(jnp.where(incl, ax[key] - arow, NEG)) * dtrow
                cm[key] = cs_ref[pl.ds(r0[key], CHUNK), :].astype(BF16)
        if with_out:
            cb = {key: _dot_nt(cm[key], jnp.concatenate([bm[key]] * HG, axis=0).astype(BF16)) for key in probs}
        for key in probs:
            bt = jnp.concatenate([bm[key], jnp.zeros_like(bm[key])], axis=0).T.astype(BF16)
            upd[key] = _dot(bt, jnp.concatenate([xw[key], jnp.zeros_like(xw[key])], axis=0))
        if with_out:
            for key in probs:
                yin[key] = _dot((cb[key] * yin[key]).astype(BF16), _stack_masked(x[key], HG).astype(BF16))
        for d in (0, 1):
            s = s_ref[d]
            for j in range(len(chunks[d])):
                key = (d, j)
                if with_out:
                    y = yin[key] + _dot(cm[key], s.astype(BF16)) * jnp.exp(ax[key])
                    yacc_ref[pl.ds(r0[key], CHUNK), :] += y
                s = s * jnp.exp(al[key]) + upd[key]
            s_ref[d] = s

    def phase(t, x_src, b_src, c_src, g_src, with_out):
        nc = t // CHUNK
        _conv_silu(x_src, t, padx_ref, wx_ref, bx_ref[...], xs_ref)
        _conv_silu(b_src, t, padb_ref, wb_ref, bb_ref[...], bs_ref)
        if with_out:
            _conv_silu(c_src, t, padb_ref, wc_ref, bcb_ref[...], cs_ref)
        gate_block(g_src, t)

        def body(i, c):
            fwd = [i * SSD_GROUP + j for j in range(SSD_GROUP)]
            bwd = [nc - 1 - i * SSD_GROUP - j for j in range(SSD_GROUP)]
            step([fwd, bwd], with_out)
            return c

        lax.fori_loop(0, nc // SSD_GROUP, body, 0)

    s_ref[...] = jnp.zeros(s_ref.shape, F32)
    yacc_ref[...] = jnp.zeros(yacc_ref.shape, F32)
    phase(t_ctx, xc_ref, bc_ref, None, gc_ref, False)
    phase(t_lat, x_ref, b_ref, c_ref, g_ref, True)

    dsk = dsk_ref[...]
    nw = nw_ref[...]

    rb = ROW_BLOCK * LANE // XW

    def out_body(i, c):
        r0 = pl.multiple_of(i * rb, rb)
        y = yacc_ref[pl.ds(r0, rb), :] + dsk * xs_ref[pl.ds(r0, rb), :]
        y = y * _silu(z_ref[pl.ds(r0, rb), :])
        ms = jnp.mean(y * y, axis=-1, keepdims=True)
        o_ref[pl.ds(r0, rb), :] = (y * lax.rsqrt(ms + EPS) * nw).astype(o_ref.dtype)
        return c

    lax.fori_loop(0, t_lat // rb, out_body, 0)


def _ssd(u_lat, u_ctx, conv_w, conv_b, gpar, dskip, nw):
    bn, t, _ = u_lat.shape
    tc = u_ctx.shape[1]
    x_blk = lambda tt: pl.BlockSpec((None, tt, XW), lambda b, g: (b, 0, g))
    n_blk = lambda tt, off: pl.BlockSpec((None, tt, LANE), lambda b, g: (b, 0, off + g))
    b_off = SSD_INNER // LANE
    c_off = b_off + SSD_BC // LANE
    z_off = SSD_CONV_CH // XW
    dt_blk = (SSD_CONV_CH + SSD_INNER) // LANE
    fixed = lambda b, g: (b, 0, dt_blk)
    return pl.pallas_call(
        functools.partial(_ssd_kernel, t_lat=t, t_ctx=tc),
        grid=(bn, SSD_GROUPS),
        in_specs=[x_blk(t), n_blk(t, b_off), n_blk(t, c_off),
                  pl.BlockSpec((None, t, XW), lambda b, g: (b, 0, z_off + g)),
                  pl.BlockSpec((None, t, LANE), fixed),
                  x_blk(tc), n_blk(tc, b_off), pl.BlockSpec((None, tc, LANE), fixed),
                  pl.BlockSpec((CONV_W, XW), lambda b, g: (0, g)),
                  pl.BlockSpec((CONV_W, LANE), lambda b, g: (0, b_off + g)),
                  pl.BlockSpec((CONV_W, LANE), lambda b, g: (0, c_off + g)),
                  pl.BlockSpec((1, XW), lambda b, g: (0, g)),
                  pl.BlockSpec((1, LANE), lambda b, g: (0, b_off + g)),
                  pl.BlockSpec((1, LANE), lambda b, g: (0, c_off + g)),
                  pl.BlockSpec((SUBLANE, LANE), lambda b, g: (0, 0)),
                  pl.BlockSpec((None, 1, XW), lambda b, g: (g, 0, 0)),
                  pl.BlockSpec((None, 1, XW), lambda b, g: (g, 0, 0))],
        out_specs=pl.BlockSpec((None, t, XW), lambda b, g: (b, 0, g)),
        out_shape=jax.ShapeDtypeStruct((bn, t, SSD_INNER), BF16),
        scratch_shapes=[pltpu.VMEM((t + 2 * SUBLANE, XW), F32),
                        pltpu.VMEM((t + 2 * SUBLANE, LANE), F32),
                        pltpu.VMEM((t, XW), F32),
                        pltpu.VMEM((t, LANE), F32),
                        pltpu.VMEM((t, LANE), F32),
                        pltpu.VMEM((t, LANE), F32),
                        pltpu.VMEM((t, XW), F32),
                        pltpu.VMEM((2, SSD_STATE, XW), F32)],
        compiler_params=pltpu.CompilerParams(dimension_semantics=("arbitrary", "arbitrary"),
                                             vmem_limit_bytes=VMEM_LIMIT),
        name="ssd",
    )(u_lat, u_lat, u_lat, u_lat, u_lat, u_ctx, u_ctx, u_ctx,
      conv_w, conv_w, conv_w, conv_b, conv_b, conv_b, gpar, dskip, nw)


def _gdn_gate_params(a_log, dt_bias):
    z = jnp.zeros((SUBLANE, LANE), F32)
    z = z.at[0, 2 * GDN_HEADS:4 * GDN_HEADS].set(a_log.reshape(-1))
    return z.at[1, 2 * GDN_HEADS:4 * GDN_HEADS].set(dt_bias.reshape(-1))


def _ssd_gate_params(a_log, dt_bias):
    z = jnp.zeros((SUBLANE, LANE), F32)
    z = z.at[0, 0:2 * SSD_HEADS].set(a_log.reshape(-1))
    return z.at[1, 0:2 * SSD_HEADS].set(dt_bias.reshape(-1))


MIX_TM = 256
TOK_TILE = 128


def _mix_kernel(go_ref, so_ref, x_ref, g1_ref, sh2_ref, sc2_ref, npm_ref, npf_ref, wo_ref, wr_ref, br_ref,
                x1_ref, h2_ref, et_ref, gt_ref):
    so = so_ref[...]
    rows = MIX_TM // GRID_W
    ssd = jnp.concatenate([so[:, r * SSD_INNER:(r + 1) * SSD_INNER] for r in range(rows)], axis=0)
    m = _dot(go_ref[...], wo_ref[0:GDN_V, :]) + _dot(ssd, wo_ref[GDN_V:GDN_V + SSD_INNER, :])
    ms = jnp.mean(m * m, axis=-1, keepdims=True)
    x1 = x_ref[...] + g1_ref[...] * (m * lax.rsqrt(ms + EPS) * npm_ref[...])
    x1_ref[...] = x1
    h2 = _modulated_norm(x1, npf_ref[...], sh2_ref[...], sc2_ref[...])
    h2_ref[...] = h2
    wr = wr_ref[...]
    h_hi = h2.astype(BF16)
    h_lo = (h2 - h_hi.astype(F32)).astype(BF16)
    w_hi = wr.astype(BF16)
    w_lo = (wr - w_hi.astype(F32)).astype(BF16)
    logits = _dot(h_hi, w_hi) + (_dot(h_hi, w_lo) + _dot(h_lo, w_hi)) + br_ref[...]
    lane = _iota(logits.shape, 1)
    lane_f = lane.astype(F32)
    l = jnp.where(lane < N_EXPERTS, logits, NEG)
    vals = []
    for _ in range(TOP_K):
        mk = jnp.max(l, axis=1, keepdims=True)
        idx = jnp.min(jnp.where(l == mk, lane_f, float(LANE)), axis=1, keepdims=True)
        l = jnp.where(lane_f == idx, NEG, l)
        vals.append(mk)
    ex = [jnp.exp(v - vals[0]) for v in vals]
    den = ex[0] + ex[1] + ex[2] + ex[3]
    gates = jnp.zeros(logits.shape, F32)
    for k in range(TOP_K):
        gates = jnp.where(lane == k, ex[k] / den, gates)
    gt_ref[...] = gates
    lt = logits.T[0:N_EXPERTS, :]
    row_f = _iota(lt.shape, 0).astype(F32)
    ids = []
    for _ in range(TOP_K):
        mk = jnp.max(lt, axis=0, keepdims=True)
        idx = jnp.min(jnp.where(lt == mk, row_f, float(N_EXPERTS)), axis=0, keepdims=True)
        lt = jnp.where(row_f == idx, NEG, lt)
        ids.append(idx.astype(I32))
    pad = jnp.full((SUBLANE - TOP_K, lt.shape[1]), -1, I32)
    et_ref[...] = jnp.concatenate(ids + [pad], axis=0)


def _mix(gdn_o, ssd_o, x, mod3, npm, npf, w_out_b16, w_router_pad, b_router_pad):
    bn, t, d = x.shape
    n = bn * t
    rows = MIX_TM // GRID_W
    nt = t // MIX_TM
    ssd_v = ssd_o.reshape(bn, GRID_W, (t // GRID_W) * SSD_INNER)
    modrow = lambda blk: pl.BlockSpec((None, 1, d), lambda b, i: (b, 0, blk))
    return pl.pallas_call(
        _mix_kernel,
        grid=(bn, nt),
        in_specs=[pl.BlockSpec((None, MIX_TM, GDN_V), lambda b, i: (b, i, 0)),
                  pl.BlockSpec((None, GRID_W, rows * SSD_INNER), lambda b, i: (b, 0, i)),
                  pl.BlockSpec((None, MIX_TM, d), lambda b, i: (b, i, 0)),
                  modrow(2), modrow(3), modrow(4),
                  pl.BlockSpec((1, d), lambda b, i: (0, 0)),
                  pl.BlockSpec((1, d), lambda b, i: (0, 0)),
                  pl.BlockSpec((GDN_V + SSD_INNER, d), lambda b, i: (0, 0)),
                  pl.BlockSpec((d, LANE), lambda b, i: (0, 0)),
                  pl.BlockSpec((1, LANE), lambda b, i: (0, 0))],
        out_specs=[pl.BlockSpec((None, MIX_TM, d), lambda b, i: (b, i, 0)),
                   pl.BlockSpec((None, MIX_TM, d), lambda b, i: (b, i, 0)),
                   pl.BlockSpec((SUBLANE, MIX_TM), lambda b, i: (0, b * nt + i)),
                   pl.BlockSpec((None, MIX_TM, LANE), lambda b, i: (b, i, 0))],
        out_shape=[jax.ShapeDtypeStruct((bn, t, d), F32),
                   jax.ShapeDtypeStruct((bn, t, d), F32),
                   jax.ShapeDtypeStruct((SUBLANE, n), I32),
                   jax.ShapeDtypeStruct((bn, t, LANE), F32)],
        compiler_params=pltpu.CompilerParams(dimension_semantics=("arbitrary", "arbitrary"),
                                             vmem_limit_bytes=VMEM_LIMIT),
        name="mix",
    )(gdn_o, ssd_v, x, mod3, mod3, mod3, npm, npf, w_out_b16, w_router_pad, b_router_pad)


def _rank_kernel(ids_ref, dest_ref, meta_ref, tril_ref, cnt_ref, rank_ref, *, n_blocks):
    nrow = ids_ref.shape[0]
    ids = ids_ref[...]
    tril_ref[...] = (_iota((nrow, nrow), 0) > _iota((nrow, nrow), 1)).astype(BF16)
    upper = (_iota((LANE, LANE), 0) < _iota((LANE, LANE), 1)).astype(BF16)
    ones = jnp.ones((LANE, LANE), BF16)
    rank_ref[...] = jnp.zeros(rank_ref.shape, F32)

    def count(e, c):
        hit = ids == e
        mask = hit.astype(F32).astype(BF16)
        before = _dot(mask, upper)
        rowsum = _dot(mask, ones)
        rows_before = _dot(tril_ref[...], rowsum.astype(BF16))
        rank_ref[...] += jnp.where(hit, before + rows_before, 0.0)
        cnt_ref[pl.ds(e, 1), :] = rows_before[nrow - 1:nrow, :] + rowsum[nrow - 1:nrow, :]
        return c

    lax.fori_loop(0, N_EXPERTS, count, 0)
    cnt = cnt_ref[...]
    padded = jnp.floor((cnt + (MOE_BLOCK - 1)) * (1.0 / MOE_BLOCK)) * MOE_BLOCK
    erow = _iota(cnt.shape, 0)
    pad_end = padded
    s = 1
    while s < N_EXPERTS:
        pad_end = pad_end + jnp.where(erow >= s, pltpu.roll(pad_end, s, 0), 0.0)
        s *= 2
    cnt_ref[...] = pad_end - padded
    jblk = (_iota((SUBLANE, LANE), 0) * LANE + _iota((SUBLANE, LANE), 1)).astype(F32) * MOE_BLOCK
    block_e = jnp.zeros((SUBLANE, LANE), F32)
    for e in range(N_EXPERTS):
        block_e = block_e + (pad_end[e:e + 1, :] <= jblk).astype(F32)
    block_e = jnp.minimum(block_e, N_EXPERTS - 1.0)
    n_used = jnp.broadcast_to(pad_end[N_EXPERTS - 1:N_EXPERTS, :] * (1.0 / MOE_BLOCK), (SUBLANE, LANE))
    meta_ref[...] = jnp.concatenate([block_e, n_used], axis=0).astype(I32)

    def place(e, c):
        rank_ref[...] += jnp.where(ids == e, cnt_ref[pl.ds(e, 1), :], 0.0)
        return c

    lax.fori_loop(0, N_EXPERTS, place, 0)
    dest_ref[...] = rank_ref[...].astype(I32)


def _rank(ids2, n_blocks):
    nrow = ids2.shape[0]
    return pl.pallas_call(
        functools.partial(_rank_kernel, n_blocks=n_blocks),
        in_specs=[pl.BlockSpec((nrow, LANE), lambda: (0, 0))],
        out_specs=[pl.BlockSpec((nrow, LANE), lambda: (0, 0)),
                   pl.BlockSpec((2 * SUBLANE, LANE), lambda: (0, 0))],
        out_shape=[jax.ShapeDtypeStruct((nrow, LANE), I32),
                   jax.ShapeDtypeStruct((2 * SUBLANE, LANE), I32)],
        scratch_shapes=[pltpu.VMEM((nrow, nrow), BF16),
                        pltpu.VMEM((N_EXPERTS, LANE), F32),
                        pltpu.VMEM((nrow, LANE), F32)],
        compiler_params=pltpu.CompilerParams(vmem_limit_bytes=VMEM_LIMIT),
        name="rank",
    )(ids2)


DISPATCH_TILES = 1
EXPERT_SPLIT = 2


def _dispatch_kernel(dest_ref, h_ref, xg_in_ref, xg_ref, sem):
    del xg_in_ref
    groups = TOK_TILE // SUBLANE

    def copy(g8, r, d):
        return pltpu.make_async_copy(h_ref.at[g8, pl.ds(r, 1), :], xg_ref.at[pl.ds(d, 1), :], sem)

    def issue(i, c):
        j = i // groups
        t8 = i % groups
        for r in range(SUBLANE):
            for k in range(TOP_K):
                copy(i, r, dest_ref[j, k, t8 * SUBLANE + r]).start(priority=k % 2)
        return c

    lax.fori_loop(0, DISPATCH_TILES * groups, issue, 0)

    def drain(i, c):
        for _ in range(SUBLANE * TOP_K):
            copy(0, 0, 0).wait()
        return c

    lax.fori_loop(0, DISPATCH_TILES * groups, drain, 0)


def _dispatch(dest3, h2, xg0):
    ntile = dest3.shape[0]
    d = h2.shape[1]
    h3 = h2.reshape(h2.shape[0] // SUBLANE, SUBLANE, d)
    groups = TOK_TILE // SUBLANE
    return pl.pallas_call(
        _dispatch_kernel,
        grid=(ntile // DISPATCH_TILES,),
        in_specs=[pl.BlockSpec((DISPATCH_TILES, TOP_K, TOK_TILE), lambda i: (i, 0, 0), memory_space=pltpu.SMEM),
                  pl.BlockSpec((DISPATCH_TILES * groups, SUBLANE, d), lambda i: (i, 0, 0)),
                  pl.BlockSpec(memory_space=pl.ANY)],
        out_specs=pl.BlockSpec(memory_space=pl.ANY),
        out_shape=jax.ShapeDtypeStruct(xg0.shape, xg0.dtype),
        scratch_shapes=[pltpu.SemaphoreType.DMA(())],
        input_output_aliases={2: 0},
        compiler_params=pltpu.CompilerParams(dimension_semantics=("arbitrary",)),
        name="dispatch",
    )(dest3, h3, xg0)


def _experts_kernel(be_ref, nu_ref, x_ref, wg_ref, wu_ref, wd_ref, bg_ref, bu_ref, bd_ref, y_ref,
                    wgb_ref, wub_ref, wdb_ref):
    j = pl.program_id(0)
    new_expert = jnp.logical_or(j == 0, be_ref[j] != be_ref[jnp.maximum(j - 1, 0)])

    @pl.when(jnp.logical_and(new_expert, j < nu_ref[0]))
    def _():
        rows = wg_ref.shape[0] // 8
        for src, dst in ((wg_ref, wgb_ref), (wu_ref, wub_ref), (wd_ref, wdb_ref)):
            def cast(i, c, src=src, dst=dst):
                r0 = pl.multiple_of(i * rows, rows)
                dst[pl.ds(r0, rows), :] = src[pl.ds(r0, rows), :].astype(BF16)
                return c
            lax.fori_loop(0, 8, cast, 0)

    @pl.when(j < nu_ref[0])
    def _():
        x = x_ref[...].astype(BF16)
        de = wgb_ref.shape[1]
        w = de // EXPERT_SPLIT
        ab = []
        for s in range(EXPERT_SPLIT):
            cols = slice(s * w, (s + 1) * w)
            ab.append((_dot(x, wgb_ref[:, cols]) + bg_ref[:, cols], _dot(x, wub_ref[:, cols]) + bu_ref[:, cols]))
        y = bd_ref[...]
        for s in range(EXPERT_SPLIT):
            a = jnp.minimum(ab[s][0], SWIGLU_LIMIT)
            b = jnp.clip(ab[s][1], -SWIGLU_LIMIT, SWIGLU_LIMIT)
            hid = a * _sigmoid(SWIGLU_ALPHA * a) * (b + 1.0)
            y = y + _dot(hid.astype(BF16), wdb_ref[s * w:(s + 1) * w, :])
        y_ref[...] = y

    @pl.when(j >= nu_ref[0])
    def _():
        y_ref[...] = jnp.zeros(y_ref.shape, y_ref.dtype)


def _experts(block_e, n_used, xg, wg, wu, wd, bg, bu, bd):
    cap, d = xg.shape
    n_blocks = cap // MOE_BLOCK
    de = wg.shape[2]
    wspec = lambda a, b: pl.BlockSpec((None, a, b), lambda j, be, nu: (be[j], 0, 0))
    grid_spec = pltpu.PrefetchScalarGridSpec(
        num_scalar_prefetch=2,
        grid=(n_blocks,),
        in_specs=[pl.BlockSpec((MOE_BLOCK, d), lambda j, be, nu: (jnp.minimum(j, nu[0] - 1), 0)),
                  wspec(d, de), wspec(d, de), wspec(de, d),
                  wspec(1, de), wspec(1, de), wspec(1, d)],
        out_specs=pl.BlockSpec((MOE_BLOCK, d), lambda j, be, nu: (j, 0)),
        scratch_shapes=[pltpu.VMEM((d, de), BF16), pltpu.VMEM((d, de), BF16), pltpu.VMEM((de, d), BF16)],
    )
    return pl.pallas_call(
        _experts_kernel,
        grid_spec=grid_spec,
        out_shape=jax.ShapeDtypeStruct((cap, d), F32),
        compiler_params=pltpu.CompilerParams(dimension_semantics=("arbitrary",),
                                             vmem_limit_bytes=VMEM_LIMIT),
        name="experts",
    )(block_e, n_used, xg, wg, wu, wd, bg, bu, bd)


def _combine_kernel(dcur_ref, dnext_ref, yg_ref, gt_ref, x1_ref, g2_ref, nw_ref, o_ref, ybuf, sems):
    i = pl.program_id(0)
    last = pl.num_programs(0) - 1
    groups = TOK_TILE // SUBLANE

    def copy(slot, k, t8, r, d):
        return pltpu.make_async_copy(yg_ref.at[pl.ds(d, 1), :], ybuf.at[slot, k, t8, pl.ds(r, 1), :],
                                     sems.at[slot])

    def issue(dref, slot):
        def body(t8, c):
            for r in range(SUBLANE):
                for k in range(TOP_K):
                    copy(slot, k, t8, r, dref[k, t8 * SUBLANE + r]).start(priority=k % 2)
            return c

        lax.fori_loop(0, groups, body, 0)

    @pl.when(i == 0)
    def _():
        issue(dcur_ref, 0)

    @pl.when(i < last)
    def _():
        issue(dnext_ref, (i + 1) % 2)

    slot = i % 2

    def drain(t8, c):
        for _ in range(SUBLANE * TOP_K):
            copy(slot, 0, 0, 0, 0).wait()
        return c

    lax.fori_loop(0, groups, drain, 0)
    d_model = o_ref.shape[-1]
    gt = gt_ref[...]
    f = gt[:, 0:1] * ybuf[slot, 0].reshape(TOK_TILE, d_model)
    for k in range(1, TOP_K):
        f = f + gt[:, k:k + 1] * ybuf[slot, k].reshape(TOK_TILE, d_model)
    ms = jnp.mean(f * f, axis=-1, keepdims=True)
    o_ref[...] = x1_ref[...] + g2_ref[...] * (f * lax.rsqrt(ms + EPS) * nw_ref[...])


def _combine(dest3, yg, gates_tok, x1, mod3, npf2, t_per_batch):
    n, d = x1.shape
    ntile = n // TOK_TILE
    per_b = t_per_batch // TOK_TILE
    return pl.pallas_call(
        _combine_kernel,
        grid=(ntile,),
        in_specs=[pl.BlockSpec((None, TOP_K, TOK_TILE), lambda i: (i, 0, 0), memory_space=pltpu.SMEM),
                  pl.BlockSpec((None, TOP_K, TOK_TILE), lambda i: (jnp.minimum(i + 1, ntile - 1), 0, 0),
                               memory_space=pltpu.SMEM),
                  pl.BlockSpec(memory_space=pl.ANY),
                  pl.BlockSpec((TOK_TILE, LANE), lambda i: (i, 0)),
                  pl.BlockSpec((TOK_TILE, d), lambda i: (i, 0)),
                  pl.BlockSpec((None, 1, d), lambda i: (i // per_b, 0, 5)),
                  pl.BlockSpec((1, d), lambda i: (0, 0))],
        out_specs=pl.BlockSpec((TOK_TILE, d), lambda i: (i, 0)),
        out_shape=jax.ShapeDtypeStruct((n, d), F32),
        scratch_shapes=[pltpu.VMEM((2, TOP_K, TOK_TILE // SUBLANE, SUBLANE, d), F32),
                        pltpu.SemaphoreType.DMA((2,))],
        compiler_params=pltpu.CompilerParams(dimension_semantics=("arbitrary",),
                                             vmem_limit_bytes=VMEM_LIMIT),
        name="combine",
    )(dest3, dest3, yg, gates_tok, x1, mod3, npf2)


def _pad_cols(w, cols):
    return jnp.pad(w, ((0, 0), (0, cols - w.shape[1])))


def kernel(x, c, ctx, c_ctx, w_ada, b_ada, norm_pre_mix, norm_post_mix, norm_pre_ffn, norm_post_ffn,
           w_in, gdn_conv_w, gdn_A_log, gdn_dt_bias, gdn_norm_w, ssd_conv_w, ssd_conv_b, ssd_A_log,
           ssd_dt_bias, ssd_D, ssd_norm_w, w_out, w_router, b_router, w_gate, b_gate, w_up, b_up,
           w_down, b_down):
    bn, t, d = x.shape
    n = bn * t
    l = 0
    mod_rows = -(-(bn + 1) // SUBLANE) * SUBLANE
    cc = jnp.zeros((mod_rows, d), F32).at[:bn].set(c).at[bn].set(c_ctx)
    mod3 = _ada(cc, w_ada[l], b_ada[l]).reshape(mod_rows, 1, 6 * d)

    w_in_b = w_in[l].astype(BF16)
    w_gdn = _pad_cols(w_in_b[:, :GDN_COLS], GDN_COLS_PAD)
    w_ssd = _pad_cols(w_in_b[:, GDN_COLS:], SSD_COLS_PAD)
    npm1 = norm_pre_mix[l].reshape(1, d)
    ug = _inproj(x, mod3, npm1, w_gdn, mod_row=None, col_major=False)
    us = _inproj(x, mod3, npm1, w_ssd, mod_row=None, col_major=True)
    ugc = _inproj(ctx, mod3, npm1, w_gdn, mod_row=bn, col_major=False)
    usc = _inproj(ctx, mod3, npm1, w_ssd, mod_row=bn, col_major=False)

    gdn_o = _gdn(ug, ugc, gdn_conv_w[l], _gdn_gate_params(gdn_A_log[l], gdn_dt_bias[l]),
                 jnp.tile(gdn_norm_w[l], 2).reshape(1, LANE))
    ssd_o = _ssd(us, usc, ssd_conv_w[l], ssd_conv_b[l].reshape(1, SSD_CONV_CH),
                 _ssd_gate_params(ssd_A_log[l], ssd_dt_bias[l]),
                 jnp.repeat(ssd_D[l], SSD_HEADDIM).reshape(SSD_GROUPS, 1, XW),
                 ssd_norm_w[l].reshape(SSD_GROUPS, 1, XW))

    x1, h2, e_t, gates_tok = _mix(
        gdn_o, ssd_o, x, mod3, norm_post_mix[l].reshape(1, d), norm_pre_ffn[l].reshape(1, d),
        w_out[l].astype(BF16), _pad_cols(w_router[l], LANE), _pad_cols(b_router[l].reshape(1, N_EXPERTS), LANE))

    n_blocks = -(-(n * TOP_K) // MOE_BLOCK) + N_EXPERTS
    cap = n_blocks * MOE_BLOCK
    ntile = n // TOK_TILE
    ids2 = e_t.reshape(SUBLANE * ntile, LANE)[:TOP_K * ntile]
    dest, meta = _rank(ids2, n_blocks)
    dest3 = dest.reshape(TOP_K, ntile, LANE).transpose(1, 0, 2)
    block_e = meta[:SUBLANE].reshape(-1)[:n_blocks]
    n_used = meta[SUBLANE, :1]

    xg = _dispatch(dest3, h2.reshape(n, d), jnp.zeros((cap, d), F32))
    yg = _experts(block_e, n_used, xg,
                  w_gate[l], w_up[l], w_down[l],
                  b_gate[l].reshape(N_EXPERTS, 1, D_EXPERT), b_up[l].reshape(N_EXPERTS, 1, D_EXPERT),
                  b_down[l].reshape(N_EXPERTS, 1, d))
    out = _combine(dest3, yg, gates_tok.reshape(n, LANE), x1.reshape(n, d), mod3,
                   norm_post_ffn[l].reshape(1, d), t)
    return out.reshape(bn, t, d)
```

```python
import functools

import jax
import jax.numpy as jnp
from jax import lax
from jax.experimental import pallas as pl
from jax.experimental.pallas import tpu as pltpu

F32 = jnp.float32
BF16 = jnp.bfloat16
I32 = jnp.int32

D_MODEL = 1024
GRID_W = 64
GDN_HEADS = 8
GDN_DK = 64
GDN_DV = 64
SSD_HEADS = 8
SSD_HEADDIM = 64
SSD_GROUPS = 2
SSD_STATE = 128
CONV_W = 5
CHUNK = 64
N_EXPERTS = 32
TOP_K = 4
D_EXPERT = 1024
SWIGLU_LIMIT = 7.0
SWIGLU_ALPHA = 1.702
MOE_BLOCK = 256
EPS = 1e-6

GDN_QK = GDN_HEADS * GDN_DK
GDN_V = GDN_HEADS * GDN_DV
GDN_CONV_CH = 2 * GDN_QK + GDN_V
GDN_COLS = GDN_CONV_CH + GDN_V + 4 * GDN_HEADS
SSD_INNER = SSD_HEADS * SSD_HEADDIM
SSD_BC = SSD_GROUPS * SSD_STATE
SSD_CONV_CH = SSD_INNER + 2 * SSD_BC
SSD_COLS = SSD_CONV_CH + SSD_INNER + 2 * SSD_HEADS

LANE = 128
SUBLANE = 8
GDN_COLS_PAD = 17 * LANE
SSD_COLS_PAD = 13 * LANE
VMEM_LIMIT = 56 * 1024 * 1024
NEG = -1e30
ROW_BLOCK = 256
INTRA_GROUP = 8
SSD_GROUP = 2


def _dot(a, b):
    return jnp.dot(a, b, preferred_element_type=F32)


def _dot_nt(a, b):
    return lax.dot_general(a, b, (((1,), (1,)), ((), ())), preferred_element_type=F32)


def _dot_f32(a, b):
    return jnp.dot(a, b, preferred_element_type=F32, precision=lax.Precision.HIGHEST)


def _silu(x):
    return x * (1.0 / (1.0 + jnp.exp(-x)))


def _sigmoid(x):
    return 1.0 / (1.0 + jnp.exp(-x))


def _softplus(x):
    return jnp.maximum(x, 0.0) + jnp.log(1.0 + jnp.exp(-jnp.abs(x)))


def _iota(shape, axis):
    return lax.broadcasted_iota(I32, shape, axis)


def _split_dot(x, ones_b16):
    hi = x.astype(BF16)
    lo = (x - hi.astype(F32)).astype(BF16)
    return _dot(hi, ones_b16) + _dot(lo, ones_b16)


def _ada_kernel(c_ref, w_ref, b_ref, o_ref):
    o_ref[...] = _dot_f32(_silu(c_ref[...]), w_ref[...]) + b_ref[...]


def _ada(cc, w_ada, b_ada):
    rows, d = cc.shape
    n = w_ada.shape[1]
    tn = 1024
    return pl.pallas_call(
        _ada_kernel,
        grid=(n // tn,),
        in_specs=[pl.BlockSpec((rows, d), lambda j: (0, 0)),
                  pl.BlockSpec((d, tn), lambda j: (0, j)),
                  pl.BlockSpec((1, tn), lambda j: (0, j))],
        out_specs=pl.BlockSpec((rows, tn), lambda j: (0, j)),
        out_shape=jax.ShapeDtypeStruct((rows, n), F32),
        compiler_params=pltpu.CompilerParams(dimension_semantics=("arbitrary",),
                                             vmem_limit_bytes=VMEM_LIMIT),
        name="ada",
    )(cc, w_ada, b_ada.reshape(1, n))


def _modulated_norm(x, nw, sh, sc):
    ms = jnp.mean(x * x, axis=-1, keepdims=True)
    return (x * lax.rsqrt(ms + EPS) * nw) * (1.0 + sc) + sh


def _inproj_kernel(x_ref, sh_ref, sc_ref, nw_ref, w_ref, o_ref, *, pieces):
    if pieces:
        xt = x_ref[...]
        x = jnp.concatenate([xt[:, i * D_MODEL:(i + 1) * D_MODEL] for i in range(pieces)], axis=0)
    else:
        x = x_ref[...]
    h = _modulated_norm(x, nw_ref[...], sh_ref[...], sc_ref[...])
    o_ref[...] = _dot(h.astype(BF16), w_ref[...])


def _inproj(x, mod3, nw, w_b16, *, mod_row, col_major):
    bn, t, d = x.shape
    cols = w_b16.shape[1]
    if col_major:
        rows = t // GRID_W
        pieces = 8
        tm = pieces * rows
        xin = x.reshape(bn, rows, GRID_W * d)
        x_spec = pl.BlockSpec((None, rows, pieces * d), lambda b, i: (b, 0, i))
    else:
        pieces = 0
        tm = min(512, t)
        xin = x
        x_spec = pl.BlockSpec((None, tm, d), lambda b, i: (b, i, 0))
    if mod_row is None:
        row = lambda b: b
    else:
        row = lambda b: mod_row
    return pl.pallas_call(
        functools.partial(_inproj_kernel, pieces=pieces),
        grid=(bn, t // tm),
        in_specs=[x_spec,
                  pl.BlockSpec((None, 1, d), lambda b, i: (row(b), 0, 0)),
                  pl.BlockSpec((None, 1, d), lambda b, i: (row(b), 0, 1)),
                  pl.BlockSpec((1, d), lambda b, i: (0, 0)),
                  pl.BlockSpec((d, cols), lambda b, i: (0, 0))],
        out_specs=pl.BlockSpec((None, tm, cols), lambda b, i: (b, i, 0)),
        out_shape=jax.ShapeDtypeStruct((bn, t, cols), F32),
        compiler_params=pltpu.CompilerParams(dimension_semantics=("arbitrary", "arbitrary"),
                                             vmem_limit_bytes=VMEM_LIMIT),
        name="inproj_cm" if col_major else "inproj",
    )(xin, mod3, mod3, nw, w_b16)


def _conv_silu(src_ref, t, pad_ref, w_ref, bias, dst_ref, post=None):
    width = src_ref.shape[-1]
    zero = jnp.zeros((SUBLANE, width), F32)
    pad_ref[0:SUBLANE, :] = zero
    pad_ref[pl.ds(SUBLANE + t, SUBLANE), :] = zero
    rb = ROW_BLOCK * LANE // width
    nb = t // rb

    def copy(i, c):
        r0 = pl.multiple_of(i * rb, rb)
        pad_ref[pl.ds(SUBLANE + r0, rb), :] = src_ref[pl.ds(r0, rb), :]
        return c

    lax.fori_loop(0, nb, copy, 0)
    w = w_ref[...]

    def body(i, c):
        r0 = pl.multiple_of(i * rb, rb)
        win = pad_ref[pl.ds(r0, rb + 2 * SUBLANE), :]
        acc = None
        for j in range(CONV_W):
            off = SUBLANE - CONV_W // 2 + j
            tap = pltpu.roll(win, rb + 2 * SUBLANE - off, 0)[0:rb, :] * w[j:j + 1, :]
            acc = tap if acc is None else acc + tap
        if bias is not None:
            acc = acc + bias
        y = _silu(acc)
        if post is not None:
            y = post(y)
        dst_ref[pl.ds(r0, rb), :] = y
        return c

    lax.fori_loop(0, nb, body, 0)


def _chunk_cumsums(x, rows_in_chunk):
    n = x.shape[0]
    pre = x
    suf = x
    s = 1
    while s < CHUNK:
        pre = pre + jnp.where(rows_in_chunk >= s, pltpu.roll(pre, s, 0), 0.0)
        suf = suf + jnp.where(rows_in_chunk < CHUNK - s, pltpu.roll(suf, n - s, 0), 0.0)
        s *= 2
    return pre, suf


def _stack_masked(x, groups):
    r, n = x.shape
    w = n // groups
    lane_group = _iota((r, n), 1) // w
    return jnp.concatenate([jnp.where(lane_group == g, x, 0.0) for g in range(groups)], axis=0)


def _gdn_kernel(q_ref, k_ref, v_ref, z_ref, g_ref, kc_ref, vc_ref, gc_ref,
                wq_ref, wk_ref, wv_ref, gpar_ref, nw_ref, o_ref,
                pad_ref, qn_ref, kn_ref, vv_ref, gb_ref, u_ref, w_ref, at_ref, qe_ref,
                m_ref, c_ref, ss_ref, egl_ref, oacc_ref, s_ref, *, t_lat, t_ctx):
    hp = pl.program_id(1)
    ones_bd = (_iota((LANE, LANE), 0) // GDN_DK == _iota((LANE, LANE), 1) // GDN_DK).astype(BF16)
    lane64 = _iota((CHUNK, LANE), 1)
    first = lane64 < GDN_DK
    ii = _iota((CHUNK, LANE), 0)
    jj = lane64 % CHUNK
    eye = (ii == jj).astype(F32)
    bd_mask = (_iota((LANE, LANE), 0) // GDN_DK) == (_iota((LANE, LANE), 1) // GDN_DV)
    gpar = gpar_ref[...]
    a_row = jnp.exp(gpar[0:1, :])
    dtb_row = gpar[1:2, :]

    def l2norm(scale):
        def f(y):
            ss = _split_dot(y * y, ones_bd)
            return y * lax.rsqrt(ss + EPS) * scale
        return f

    def gate_block(src_ref, t):
        lane = _iota((ROW_BLOCK, LANE), 1)
        ric = _iota((ROW_BLOCK, LANE), 0) % CHUNK
        shift = (LANE - 2 * hp) % LANE

        def body(i, c):
            r0 = pl.multiple_of(i * ROW_BLOCK, ROW_BLOCK)
            x = src_ref[pl.ds(r0, ROW_BLOCK), :]
            act = jnp.where(lane < 2 * GDN_HEADS, _sigmoid(x), -a_row * _softplus(x + dtb_row))
            pre, suf = _chunk_cumsums(act, ric)
            blk = jnp.where(lane < 2 * GDN_HEADS, act, jnp.where(lane < 3 * GDN_HEADS, pre, suf))
            gb_ref[pl.ds(r0, ROW_BLOCK), :] = pltpu.roll(blk, shift, 1)
            return c

        lax.fori_loop(0, t // ROW_BLOCK, body, 0)

    def expand(tile, c0):
        return jnp.where(first, tile[:, c0:c0 + 1], tile[:, c0 + 1:c0 + 2])

    first2 = jnp.concatenate([first, first], axis=1)
    dirs = (0, 1)

    def intra(ns, with_out):
        r0s = [pl.multiple_of(n * CHUNK, CHUNK) for n in ns]
        ks = [kn_ref[pl.ds(r0, CHUNK), :] for r0 in r0s]
        gs = [gb_ref[pl.ds(r0, CHUNK), :] for r0 in r0s]
        gts = [jnp.concatenate([g, g], axis=0).T for g in gs]
        k_bds = [_stack_masked(k, 2).astype(BF16) for k in ks]
        probs = [(c, d) for c in range(len(ns)) for d in dirs]
        beta, gx, rel, gl, kb = {}, {}, {}, {}, {}
        for c, d in probs:
            g = gs[c]
            c0 = 2 * GDN_HEADS + GDN_HEADS * d
            beta[c, d] = expand(g, GDN_HEADS * d)
            gx[c, d] = expand(g, c0)
            grow = jnp.where(first[0:1, :], gts[c][c0:c0 + 1, :], gts[c][c0 + 1:c0 + 2, :])
            rel[c, d] = gx[c, d] - grow
            gl[c, d] = gx[c, d][CHUNK - 1:CHUNK, :] if d == 0 else gx[c, d][0:1, :]
            kb[c, d] = ks[c] * beta[c, d]
        strict = {0: ii > jj, 1: ii < jj}
        incl = {0: ii >= jj, 1: ii <= jj}
        lhs = []
        for c in range(len(ns)):
            parts = [kb[c, 0], kb[c, 1]]
            if with_out:
                parts.append(qn_ref[pl.ds(r0s[c], CHUNK), :])
            lhs.append(jnp.concatenate(parts, axis=0).astype(BF16))
        kk = [_dot_nt(lhs[c], k_bds[c]) for c in range(len(ns))]
        nm, p = {}, {}
        for c, d in probs:
            a = kk[c][d * CHUNK:(d + 1) * CHUNK, :] * jnp.exp(jnp.where(strict[d], rel[c, d], NEG))
            nm[c, d] = -a
            p[c, d] = eye + nm[c, d]
        for key in probs:
            nm[key] = _dot(nm[key].astype(BF16), _stack_masked(nm[key], 2).astype(BF16))
        for _ in range(4):
            for key in probs:
                both = _dot(jnp.concatenate([nm[key], p[key]], axis=0).astype(BF16),
                            _stack_masked(nm[key], 2).astype(BF16))
                nm[key] = both[0:CHUNK, :]
                p[key] = p[key] + both[CHUNK:2 * CHUNK, :]
        for key in probs:
            p[key] = p[key] + _dot(p[key].astype(BF16), _stack_masked(nm[key], 2).astype(BF16))
        vs = [vv_ref[pl.ds(r0, CHUNK), :] for r0 in r0s]
        eg, uw = {}, {}
        for c, d in probs:
            eg[c, d] = jnp.exp(gx[c, d])
            rhs = jnp.concatenate([vs[c] * beta[c, d], kb[c, d] * eg[c, d]], axis=1)
            rhs_bd = jnp.concatenate([jnp.where(first2, rhs, 0.0), jnp.where(first2, 0.0, rhs)], axis=0)
            uw[c, d] = _dot(p[c, d].astype(BF16), rhs_bd.astype(BF16))
        mc = {}
        for c, d in probs:
            kdec = ks[c] * jnp.exp(gl[c, d] - gx[c, d])
            kdt = jnp.concatenate([kdec, jnp.zeros_like(kdec)], axis=0).T
            wu = jnp.concatenate([uw[c, d][:, LANE:], uw[c, d][:, :LANE]], axis=1).astype(BF16)
            mc[c, d] = _dot(kdt.astype(BF16), jnp.concatenate([wu, jnp.zeros_like(wu)], axis=0))
        for c, d in probs:
            r0, n = r0s[c], ns[c]
            m0 = pl.multiple_of(n * LANE, LANE)
            m_ref[d, pl.ds(m0, LANE), :] = jnp.where(bd_mask, mc[c, d][:, :LANE], 0.0).astype(BF16)
            c_ref[d, pl.ds(m0, LANE), :] = jnp.where(bd_mask, mc[c, d][:, LANE:], 0.0)
            egl_ref[d, n] = jnp.broadcast_to(jnp.exp(gl[c, d]), (SUBLANE, LANE))
            if with_out:
                u_ref[d, pl.ds(r0, CHUNK), :] = uw[c, d][:, :LANE]
                w_ref[d, pl.ds(r0, CHUNK), :] = uw[c, d][:, LANE:].astype(BF16)
                at = kk[c][2 * CHUNK:3 * CHUNK, :] * jnp.exp(jnp.where(incl[d], rel[c, d], NEG))
                at_ref[d, pl.ds(r0, CHUNK), :] = at.astype(BF16)
                qe_ref[d, pl.ds(r0, CHUNK), :] = (qn_ref[pl.ds(r0, CHUNK), :] * eg[c, d]).astype(BF16)

    def scan_step(ns, with_out):
        m0s = [pl.multiple_of(n * LANE, LANE) for n in ns]
        s = [s_ref[d] for d in dirs]
        sb = [s[d].astype(BF16) for d in dirs]
        ms = [_dot(m_ref[d, pl.ds(m0s[d], LANE), :], sb[d]) for d in dirs]
        for d in dirs:
            s_ref[d] = s[d] * egl_ref[d, ns[d]][0:1, :] - ms[d] + c_ref[d, pl.ds(m0s[d], LANE), :]
            if with_out:
                ss_ref[d, pl.ds(m0s[d], LANE), :] = sb[d]

    def outputs(ns):
        cs = range(len(ns))
        probs = [(c, d) for c in cs for d in dirs]
        r0 = [pl.multiple_of(n * CHUNK, CHUNK) for n in ns]
        m0 = [pl.multiple_of(n * LANE, LANE) for n in ns]
        ws = {(c, d): _dot(jnp.concatenate([w_ref[d, pl.ds(r0[c], CHUNK), :], qe_ref[d, pl.ds(r0[c], CHUNK), :]],
                                           axis=0), ss_ref[d, pl.ds(m0[c], LANE), :]) for c, d in probs}
        vnew = {(c, d): u_ref[d, pl.ds(r0[c], CHUNK), :] - ws[c, d][0:CHUNK, :] for c, d in probs}
        av = {(c, d): _dot(at_ref[d, pl.ds(r0[c], CHUNK), :], _stack_masked(vnew[c, d], 2).astype(BF16))
              for c, d in probs}
        for c in cs:
            oacc_ref[pl.ds(r0[c], CHUNK), :] = (ws[c, 0][CHUNK:2 * CHUNK, :] + av[c, 0]
                                                + ws[c, 1][CHUNK:2 * CHUNK, :] + av[c, 1])

    def phase(t, k_src, v_src, g_src, q_src, with_out):
        nc = t // CHUNK
        _conv_silu(k_src, t, pad_ref, wk_ref, None, kn_ref, l2norm(1.0))
        _conv_silu(v_src, t, pad_ref, wv_ref, None, vv_ref)
        if with_out:
            _conv_silu(q_src, t, pad_ref, wq_ref, None, qn_ref, l2norm(GDN_DK ** -0.5))
        gate_block(g_src, t)

        group = min(INTRA_GROUP, nc)

        def intra_body(i, c):
            intra([i * group + j for j in range(group)], with_out)
            return c

        lax.fori_loop(0, nc // group, intra_body, 0)

        def scan_body(i, c):
            scan_step([i, nc - 1 - i], with_out)
            return c

        lax.fori_loop(0, nc, scan_body, 0)
        if with_out:
            def out_group(i, c):
                outputs([i * group + j for j in range(group)])
                return c

            lax.fori_loop(0, nc // group, out_group, 0)

    s_ref[...] = jnp.zeros(s_ref.shape, F32)
    phase(t_ctx, kc_ref, vc_ref, gc_ref, None, False)
    phase(t_lat, k_ref, v_ref, g_ref, q_ref, True)

    nw = nw_ref[...]

    def out_body(i, c):
        r0 = pl.multiple_of(i * ROW_BLOCK, ROW_BLOCK)
        o = oacc_ref[pl.ds(r0, ROW_BLOCK), :]
        ms = _split_dot(o * o, ones_bd) * (1.0 / GDN_DV)
        y = o * lax.rsqrt(ms + EPS) * nw * _silu(z_ref[pl.ds(r0, ROW_BLOCK), :])
        o_ref[pl.ds(r0, ROW_BLOCK), :] = y.astype(o_ref.dtype)
        return c

    lax.fori_loop(0, t_lat // ROW_BLOCK, out_body, 0)


def _gdn(u_lat, u_ctx, conv_w, gpar, nw2):
    bn, t, _ = u_lat.shape
    tc = u_ctx.shape[1]
    npairs = GDN_HEADS // 2
    nc = t // CHUNK
    col = lambda off: (lambda b, h: (b, 0, off + h))
    lat = lambda off: pl.BlockSpec((None, t, LANE), col(off))
    ctx = lambda off: pl.BlockSpec((None, tc, LANE), col(off))
    gate_blk = GDN_CONV_CH // LANE + GDN_V // LANE
    fixed = lambda b, h: (b, 0, gate_blk)
    cw = lambda off: pl.BlockSpec((CONV_W, LANE), lambda b, h: (0, off + h))
    return pl.pallas_call(
        functools.partial(_gdn_kernel, t_lat=t, t_ctx=tc),
        grid=(bn, npairs),
        in_specs=[lat(0), lat(npairs), lat(2 * npairs), lat(3 * npairs),
                  pl.BlockSpec((None, t, LANE), fixed),
                  ctx(npairs), ctx(2 * npairs), pl.BlockSpec((None, tc, LANE), fixed),
                  cw(0), cw(npairs), cw(2 * npairs),
                  pl.BlockSpec((SUBLANE, LANE), lambda b, h: (0, 0)),
                  pl.BlockSpec((1, LANE), lambda b, h: (0, 0))],
        out_specs=pl.BlockSpec((None, t, LANE), lambda b, h: (b, 0, h)),
        out_shape=jax.ShapeDtypeStruct((bn, t, GDN_V), BF16),
        scratch_shapes=[pltpu.VMEM((t + 2 * SUBLANE, LANE), F32),
                        pltpu.VMEM((t, LANE), F32),
                        pltpu.VMEM((t, LANE), F32),
                        pltpu.VMEM((t, LANE), F32),
                        pltpu.VMEM((t, LANE), F32),
                        pltpu.VMEM((2, t, LANE), F32),
                        pltpu.VMEM((2, t, LANE), BF16),
                        pltpu.VMEM((2, t, LANE), BF16),
                        pltpu.VMEM((2, t, LANE), BF16),
                        pltpu.VMEM((2, nc * LANE, LANE), BF16),
                        pltpu.VMEM((2, nc * LANE, LANE), F32),
                        pltpu.VMEM((2, nc * LANE, LANE), BF16),
                        pltpu.VMEM((2, nc, SUBLANE, LANE), F32),
                        pltpu.VMEM((t, LANE), F32),
                        pltpu.VMEM((2, LANE, LANE), F32)],
        compiler_params=pltpu.CompilerParams(dimension_semantics=("arbitrary", "arbitrary"),
                                             vmem_limit_bytes=VMEM_LIMIT),
        name="gdn",
    )(u_lat, u_lat, u_lat, u_lat, u_lat, u_ctx, u_ctx, u_ctx, conv_w, conv_w, conv_w, gpar, nw2)


HG = SSD_HEADS // SSD_GROUPS
XW = HG * SSD_HEADDIM


def _ssd_kernel(x_ref, b_ref, c_ref, z_ref, g_ref, xc_ref, bc_ref, gc_ref,
                wx_ref, wb_ref, wc_ref, bx_ref, bb_ref, bcb_ref, gpar_ref, dsk_ref, nw_ref, o_ref,
                padx_ref, padb_ref, xs_ref, bs_ref, cs_ref, gb_ref, yacc_ref, s_ref, *, t_lat, t_ctx):
    grp = pl.program_id(1)
    lane4 = _iota((CHUNK, XW), 1)
    lg = lane4 // SSD_HEADDIM
    ii = _iota((CHUNK, XW), 0)
    jj = lane4 % CHUNK
    gpar = gpar_ref[...]
    a_row = -jnp.exp(gpar[0:1, :])
    dtb_row = gpar[1:2, :]

    def gate_block(src_ref, t):
        lane = _iota((ROW_BLOCK, LANE), 1)
        ric = _iota((ROW_BLOCK, LANE), 0) % CHUNK
        shift = (LANE - HG * grp) % LANE

        def body(i, c):
            r0 = pl.multiple_of(i * ROW_BLOCK, ROW_BLOCK)
            x = src_ref[pl.ds(r0, ROW_BLOCK), :]
            dt = _softplus(x + dtb_row)
            a = pltpu.roll(dt * a_row, 2 * SSD_HEADS, 1)
            pre, suf = _chunk_cumsums(a, ric)
            blk = jnp.where(lane < 2 * SSD_HEADS, dt, jnp.where(lane < 3 * SSD_HEADS, pre, suf))
            gb_ref[pl.ds(r0, ROW_BLOCK), :] = pltpu.roll(blk, shift, 1)
            return c

        lax.fori_loop(0, t // ROW_BLOCK, body, 0)

    def expand(tile, c0):
        out = tile[:, c0 + HG - 1:c0 + HG]
        for h in range(HG - 2, -1, -1):
            out = jnp.where(lg == h, tile[:, c0 + h:c0 + h + 1], out)
        return out

    def rowsel(gt2, c0):
        out = gt2[c0 + HG - 1:c0 + HG, :]
        for h in range(HG - 2, -1, -1):
            out = jnp.where(lg[0:1, :] == h, gt2[c0 + h:c0 + h + 1, :], out)
        return out

    def step(chunks, with_out):
        probs = [(d, j) for d in (0, 1) for j in range(len(chunks[d]))]
        r0, ax, al, x, bm, cm, xw, upd, yin = {}, {}, {}, {}, {}, {}, {}, {}, {}
        for key in probs:
            d, j = key
            r0[key] = pl.multiple_of(chunks[d][j] * CHUNK, CHUNK)
            g = gb_ref[pl.ds(r0[key], CHUNK), :]
            dtx = expand(g, SSD_HEADS * d)
            ax[key] = expand(g, 2 * SSD_HEADS + SSD_HEADS * d)
            al[key] = ax[key][CHUNK - 1:CHUNK, :] if d == 0 else ax[key][0:1, :]
            x[key] = xs_ref[pl.ds(r0[key], CHUNK), :]
            bm[key] = bs_ref[pl.ds(r0[key], CHUNK), :]
            xw[key] = (x[key] * (jnp.exp(al[key] - ax[key]) * dtx)).astype(BF16)
            if with_out:
                gt = jnp.concatenate([g, g], axis=0).T
                gt2 = jnp.concatenate([gt, gt], axis=1)
                arow = rowsel(gt2, 2 * SSD_HEADS + SSD_HEADS * d)
                dtrow = rowsel(gt2, SSD_HEADS * d)
                incl = (ii >= jj) if d == 0 else (ii <= jj)
                yin[key] = jnp.exp(jnp.where(incl, ax[key] - arow, NEG)) * dtrow
                cm[key] = cs_ref[pl.ds(r0[key], CHUNK), :].astype(BF16)
        if with_out:
            cb = {key: _dot_nt(cm[key], jnp.concatenate([bm[key]] * HG, axis=0).astype(BF16)) for key in probs}
        for key in probs:
            bt = jnp.concatenate([bm[key], jnp.zeros_like(bm[key])], axis=0).T.astype(BF16)
            upd[key] = _dot(bt, jnp.concatenate([xw[key], jnp.zeros_like(xw[key])], axis=0))
        if with_out:
            for key in probs:
                yin[key] = _dot((cb[key] * yin[key]).astype(BF16), _stack_masked(x[key], HG).astype(BF16))
        for d in (0, 1):
            s = s_ref[d]
            for j in range(len(chunks[d])):
                key = (d, j)
                if with_out:
                    y = yin[key] + _dot(cm[key], s.astype(BF16)) * jnp.exp(ax[key])
                    yacc_ref[pl.ds(r0[key], CHUNK), :] += y
                s = s * jnp.exp(al[key]) + upd[key]
            s_ref[d] = s

    def phase(t, x_src, b_src, c_src, g_src, with_out):
        nc = t // CHUNK
        _conv_silu(x_src, t, padx_ref, wx_ref, bx_ref[...], xs_ref)
        _conv_silu(b_src, t, padb_ref, wb_ref, bb_ref[...], bs_ref)
        if with_out:
            _conv_silu(c_src, t, padb_ref, wc_ref, bcb_ref[...], cs_ref)
        gate_block(g_src, t)

        def body(i, c):
            fwd = [i * SSD_GROUP + j for j in range(SSD_GROUP)]
            bwd = [nc - 1 - i * SSD_GROUP - j for j in range(SSD_GROUP)]
            step([fwd, bwd], with_out)
            return c

        lax.fori_loop(0, nc // SSD_GROUP, body, 0)

    s_ref[...] = jnp.zeros(s_ref.shape, F32)
    yacc_ref[...] = jnp.zeros(yacc_ref.shape, F32)
    phase(t_ctx, xc_ref, bc_ref, None, gc_ref, False)
    phase(t_lat, x_ref, b_ref, c_ref, g_ref, True)

    dsk = dsk_ref[...]
    nw = nw_ref[...]

    rb = ROW_BLOCK * LANE // XW

    def out_body(i, c):
        r0 = pl.multiple_of(i * rb, rb)
        y = yacc_ref[pl.ds(r0, rb), :] + dsk * xs_ref[pl.ds(r0, rb), :]
        y = y * _silu(z_ref[pl.ds(r0, rb), :])
        ms = jnp.mean(y * y, axis=-1, keepdims=True)
        o_ref[pl.ds(r0, rb), :] = (y * lax.rsqrt(ms + EPS) * nw).astype(o_ref.dtype)
        return c

    lax.fori_loop(0, t_lat // rb, out_body, 0)


def _ssd(u_lat, u_ctx, conv_w, conv_b, gpar, dskip, nw):
    bn, t, _ = u_lat.shape
    tc = u_ctx.shape[1]
    x_blk = lambda tt: pl.BlockSpec((None, tt, XW), lambda b, g: (b, 0, g))
    n_blk = lambda tt, off: pl.BlockSpec((None, tt, LANE), lambda b, g: (b, 0, off + g))
    b_off = SSD_INNER // LANE
    c_off = b_off + SSD_BC // LANE
    z_off = SSD_CONV_CH // XW
    dt_blk = (SSD_CONV_CH + SSD_INNER) // LANE
    fixed = lambda b, g: (b, 0, dt_blk)
    return pl.pallas_call(
        functools.partial(_ssd_kernel, t_lat=t, t_ctx=tc),
        grid=(bn, SSD_GROUPS),
        in_specs=[x_blk(t), n_blk(t, b_off), n_blk(t, c_off),
                  pl.BlockSpec((None, t, XW), lambda b, g: (b, 0, z_off + g)),
                  pl.BlockSpec((None, t, LANE), fixed),
                  x_blk(tc), n_blk(tc, b_off), pl.BlockSpec((None, tc, LANE), fixed),
                  pl.BlockSpec((CONV_W, XW), lambda b, g: (0, g)),
                  pl.BlockSpec((CONV_W, LANE), lambda b, g: (0, b_off + g)),
                  pl.BlockSpec((CONV_W, LANE), lambda b, g: (0, c_off + g)),
                  pl.BlockSpec((1, XW), lambda b, g: (0, g)),
                  pl.BlockSpec((1, LANE), lambda b, g: (0, b_off + g)),
                  pl.BlockSpec((1, LANE), lambda b, g: (0, c_off + g)),
                  pl.BlockSpec((SUBLANE, LANE), lambda b, g: (0, 0)),
                  pl.BlockSpec((None, 1, XW), lambda b, g: (g, 0, 0)),
                  pl.BlockSpec((None, 1, XW), lambda b, g: (g, 0, 0))],
        out_specs=pl.BlockSpec((None, t, XW), lambda b, g: (b, 0, g)),
        out_shape=jax.ShapeDtypeStruct((bn, t, SSD_INNER), BF16),
        scratch_shapes=[pltpu.VMEM((t + 2 * SUBLANE, XW), F32),
                        pltpu.VMEM((t + 2 * SUBLANE, LANE), F32),
                        pltpu.VMEM((t, XW), F32),
                        pltpu.VMEM((t, LANE), F32),
                        pltpu.VMEM((t, LANE), F32),
                        pltpu.VMEM((t, LANE), F32),
                        pltpu.VMEM((t, XW), F32),
                        pltpu.VMEM((2, SSD_STATE, XW), F32)],
        compiler_params=pltpu.CompilerParams(dimension_semantics=("arbitrary", "arbitrary"),
                                             vmem_limit_bytes=VMEM_LIMIT),
        name="ssd",
    )(u_lat, u_lat, u_lat, u_lat, u_lat, u_ctx, u_ctx, u_ctx,
      conv_w, conv_w, conv_w, conv_b, conv_b, conv_b, gpar, dskip, nw)


def _gdn_gate_params(a_log, dt_bias):
    z = jnp.zeros((SUBLANE, LANE), F32)
    z = z.at[0, 2 * GDN_HEADS:4 * GDN_HEADS].set(a_log.reshape(-1))
    return z.at[1, 2 * GDN_HEADS:4 * GDN_HEADS].set(dt_bias.reshape(-1))


def _ssd_gate_params(a_log, dt_bias):
    z = jnp.zeros((SUBLANE, LANE), F32)
    z = z.at[0, 0:2 * SSD_HEADS].set(a_log.reshape(-1))
    return z.at[1, 0:2 * SSD_HEADS].set(dt_bias.reshape(-1))


MIX_TM = 256
TOK_TILE = 128


def _mix_kernel(go_ref, so_ref, x_ref, g1_ref, sh2_ref, sc2_ref, npm_ref, npf_ref, wo_ref, wr_ref, br_ref,
                x1_ref, h2_ref, et_ref, gt_ref):
    so = so_ref[...]
    rows = MIX_TM // GRID_W
    ssd = jnp.concatenate([so[:, r * SSD_INNER:(r + 1) * SSD_INNER] for r in range(rows)], axis=0)
    m = _dot(go_ref[...], wo_ref[0:GDN_V, :]) + _dot(ssd, wo_ref[GDN_V:GDN_V + SSD_INNER, :])
    ms = jnp.mean(m * m, axis=-1, keepdims=True)
    x1 = x_ref[...] + g1_ref[...] * (m * lax.rsqrt(ms + EPS) * npm_ref[...])
    x1_ref[...] = x1
    h2 = _modulated_norm(x1, npf_ref[...], sh2_ref[...], sc2_ref[...])
    h2_ref[...] = h2.reshape(h2_ref.shape)
    wr = wr_ref[...]
    h_hi = h2.astype(BF16)
    h_lo = (h2 - h_hi.astype(F32)).astype(BF16)
    w_hi = wr.astype(BF16)
    w_lo = (wr - w_hi.astype(F32)).astype(BF16)
    logits = _dot(h_hi, w_hi) + (_dot(h_hi, w_lo) + _dot(h_lo, w_hi)) + br_ref[...]
    lane = _iota(logits.shape, 1)
    lane_f = lane.astype(F32)
    l = jnp.where(lane < N_EXPERTS, logits, NEG)
    vals = []
    for _ in range(TOP_K):
        mk = jnp.max(l, axis=1, keepdims=True)
        idx = jnp.min(jnp.where(l == mk, lane_f, float(LANE)), axis=1, keepdims=True)
        l = jnp.where(lane_f == idx, NEG, l)
        vals.append(mk)
    ex = [jnp.exp(v - vals[0]) for v in vals]
    den = ex[0] + ex[1] + ex[2] + ex[3]
    gates = jnp.zeros(logits.shape, F32)
    for k in range(TOP_K):
        gates = jnp.where(lane == k, ex[k] / den, gates)
    gt_ref[...] = gates
    lt = logits.T[0:N_EXPERTS, :]
    row_f = _iota(lt.shape, 0).astype(F32)
    ids = []
    for _ in range(TOP_K):
        mk = jnp.max(lt, axis=0, keepdims=True)
        idx = jnp.min(jnp.where(lt == mk, row_f, float(N_EXPERTS)), axis=0, keepdims=True)
        lt = jnp.where(row_f == idx, NEG, lt)
        ids.append(idx.astype(I32))
    pad = jnp.full((SUBLANE - TOP_K, lt.shape[1]), -1, I32)
    et_ref[...] = jnp.concatenate(ids + [pad], axis=0)


def _mix(gdn_o, ssd_o, x, mod3, npm, npf, w_out_b16, w_router_pad, b_router_pad):
    bn, t, d = x.shape
    n = bn * t
    rows = MIX_TM // GRID_W
    nt = t // MIX_TM
    ssd_v = ssd_o.reshape(bn, GRID_W, (t // GRID_W) * SSD_INNER)
    modrow = lambda blk: pl.BlockSpec((None, 1, d), lambda b, i: (b, 0, blk))
    return pl.pallas_call(
        _mix_kernel,
        grid=(bn, nt),
        in_specs=[pl.BlockSpec((None, MIX_TM, GDN_V), lambda b, i: (b, i, 0)),
                  pl.BlockSpec((None, GRID_W, rows * SSD_INNER), lambda b, i: (b, 0, i)),
                  pl.BlockSpec((None, MIX_TM, d), lambda b, i: (b, i, 0)),
                  modrow(2), modrow(3), modrow(4),
                  pl.BlockSpec((1, d), lambda b, i: (0, 0)),
                  pl.BlockSpec((1, d), lambda b, i: (0, 0)),
                  pl.BlockSpec((GDN_V + SSD_INNER, d), lambda b, i: (0, 0)),
                  pl.BlockSpec((d, LANE), lambda b, i: (0, 0)),
                  pl.BlockSpec((1, LANE), lambda b, i: (0, 0))],
        out_specs=[pl.BlockSpec((None, MIX_TM, d), lambda b, i: (b, i, 0)),
                   pl.BlockSpec((MIX_TM, SUBLANE, d // SUBLANE), lambda b, i: (b * nt + i, 0, 0)),
                   pl.BlockSpec((SUBLANE, MIX_TM), lambda b, i: (0, b * nt + i)),
                   pl.BlockSpec((None, MIX_TM, LANE), lambda b, i: (b, i, 0))],
        out_shape=[jax.ShapeDtypeStruct((bn, t, d), F32),
                   jax.ShapeDtypeStruct((n, SUBLANE, d // SUBLANE), F32),
                   jax.ShapeDtypeStruct((SUBLANE, n), I32),
                   jax.ShapeDtypeStruct((bn, t, LANE), F32)],
        compiler_params=pltpu.CompilerParams(dimension_semantics=("arbitrary", "arbitrary"),
                                             vmem_limit_bytes=VMEM_LIMIT),
        name="mix",
    )(gdn_o, ssd_v, x, mod3, mod3, mod3, npm, npf, w_out_b16, w_router_pad, b_router_pad)


def _rank_kernel(ids_ref, dest_ref, meta_ref, tril_ref, cnt_ref, rank_ref, *, n_blocks):
    nrow = ids_ref.shape[0]
    ids = ids_ref[...]
    tril_ref[...] = (_iota((nrow, nrow), 0) > _iota((nrow, nrow), 1)).astype(BF16)
    upper = (_iota((LANE, LANE), 0) < _iota((LANE, LANE), 1)).astype(BF16)
    ones = jnp.ones((LANE, LANE), BF16)
    rank_ref[...] = jnp.zeros(rank_ref.shape, F32)

    def count(e, c):
        hit = ids == e
        mask = hit.astype(F32).astype(BF16)
        before = _dot(mask, upper)
        rowsum = _dot(mask, ones)
        rows_before = _dot(tril_ref[...], rowsum.astype(BF16))
        rank_ref[...] += jnp.where(hit, before + rows_before, 0.0)
        cnt_ref[pl.ds(e, 1), :] = rows_before[nrow - 1:nrow, :] + rowsum[nrow - 1:nrow, :]
        return c

    lax.fori_loop(0, N_EXPERTS, count, 0)
    cnt = cnt_ref[...]
    padded = jnp.floor((cnt + (MOE_BLOCK - 1)) * (1.0 / MOE_BLOCK)) * MOE_BLOCK
    erow = _iota(cnt.shape, 0)
    pad_end = padded
    s = 1
    while s < N_EXPERTS:
        pad_end = pad_end + jnp.where(erow >= s, pltpu.roll(pad_end, s, 0), 0.0)
        s *= 2
    cnt_ref[...] = pad_end - padded
    jblk = (_iota((SUBLANE, LANE), 0) * LANE + _iota((SUBLANE, LANE), 1)).astype(F32) * MOE_BLOCK
    block_e = jnp.zeros((SUBLANE, LANE), F32)
    for e in range(N_EXPERTS):
        block_e = block_e + (pad_end[e:e + 1, :] <= jblk).astype(F32)
    block_e = jnp.minimum(block_e, N_EXPERTS - 1.0)
    n_used = pad_end[N_EXPERTS - 1:N_EXPERTS, :] * (1.0 / MOE_BLOCK)
    on_diag = erow == _iota(cnt.shape, 1)
    pad_end_l = jnp.sum(jnp.where(on_diag, pad_end, 0.0), axis=0, keepdims=True)
    cnt_l = jnp.sum(jnp.where(on_diag, cnt, 0.0), axis=0, keepdims=True)
    r8 = _iota((SUBLANE, LANE), 0)
    tail = jnp.where(r8 == 0, n_used, jnp.where(r8 == 1, pad_end_l, jnp.where(r8 == 2, cnt_l, 0.0)))
    meta_ref[...] = jnp.concatenate([block_e, tail], axis=0).astype(I32)

    def place(e, c):
        rank_ref[...] += jnp.where(ids == e, cnt_ref[pl.ds(e, 1), :], 0.0)
        return c

    lax.fori_loop(0, N_EXPERTS, place, 0)
    dest_ref[...] = rank_ref[...].astype(I32)


def _rank(ids2, n_blocks):
    nrow = ids2.shape[0]
    return pl.pallas_call(
        functools.partial(_rank_kernel, n_blocks=n_blocks),
        in_specs=[pl.BlockSpec((nrow, LANE), lambda: (0, 0))],
        out_specs=[pl.BlockSpec((nrow, LANE), lambda: (0, 0)),
                   pl.BlockSpec((2 * SUBLANE, LANE), lambda: (0, 0))],
        out_shape=[jax.ShapeDtypeStruct((nrow, LANE), I32),
                   jax.ShapeDtypeStruct((2 * SUBLANE, LANE), I32)],
        scratch_shapes=[pltpu.VMEM((nrow, nrow), BF16),
                        pltpu.VMEM((N_EXPERTS, LANE), F32),
                        pltpu.VMEM((nrow, LANE), F32)],
        compiler_params=pltpu.CompilerParams(vmem_limit_bytes=VMEM_LIMIT),
        name="rank",
    )(ids2)


EXPERT_SPLIT = 2


ISSUE_UNROLL = 8


META_N_USED = SUBLANE
META_PAD_END = SUBLANE + 1
META_COUNT = SUBLANE + 2


def _dispatch_kernel(meta_ref, dest_ref, h_ref, xg_ref, zero_ref, sem, zsem):
    n_blocks = xg_ref.shape[0] // MOE_BLOCK

    def zero_copy(start):
        return pltpu.make_async_copy(zero_ref, xg_ref.at[pl.ds(start, MOE_BLOCK)], zsem)

    def zero_fill(action):
        for e in range(N_EXPERTS):
            @pl.when(meta_ref[META_COUNT, e] > 0)
            def _(e=e):
                action(zero_copy(meta_ref[META_PAD_END, e] - MOE_BLOCK))
        for j in range(n_blocks - N_EXPERTS, n_blocks):
            @pl.when(j >= meta_ref[META_N_USED, 0])
            def _(j=j):
                action(zero_copy(j * MOE_BLOCK))

    @pl.when(pl.program_id(0) == 0)
    def _():
        zero_ref[...] = jnp.zeros(zero_ref.shape, zero_ref.dtype)
        zero_fill(lambda cp: cp.start())
        zero_fill(lambda cp: cp.wait())

    def copy(t, d):
        return pltpu.make_async_copy(h_ref.at[t], xg_ref.at[d], sem)

    def issue(i, c):
        for r in range(ISSUE_UNROLL):
            t = i * ISSUE_UNROLL + r
            for k in range(TOP_K):
                copy(t, dest_ref[k, t]).start(priority=k % 2)
        return c

    lax.fori_loop(0, TOK_TILE // ISSUE_UNROLL, issue, 0)

    def drain(i, c):
        for _ in range(ISSUE_UNROLL * TOP_K):
            copy(0, 0).wait()
        return c

    lax.fori_loop(0, TOK_TILE // ISSUE_UNROLL, drain, 0)


def _dispatch(meta, dest3, h3, cap):
    ntile = dest3.shape[0]
    row = h3.shape[1:]
    return pl.pallas_call(
        _dispatch_kernel,
        grid=(ntile,),
        in_specs=[pl.BlockSpec(meta.shape, lambda i: (0, 0), memory_space=pltpu.SMEM),
                  pl.BlockSpec((None, TOP_K, TOK_TILE), lambda i: (i, 0, 0), memory_space=pltpu.SMEM),
                  pl.BlockSpec((TOK_TILE,) + row, lambda i: (i, 0, 0))],
        out_specs=pl.BlockSpec(memory_space=pl.ANY),
        out_shape=jax.ShapeDtypeStruct((cap,) + row, h3.dtype),
        scratch_shapes=[pltpu.VMEM((MOE_BLOCK,) + row, h3.dtype),
                        pltpu.SemaphoreType.DMA(()),
                        pltpu.SemaphoreType.DMA(())],
        compiler_params=pltpu.CompilerParams(dimension_semantics=("arbitrary",)),
        name="dispatch",
    )(meta, dest3, h3)


def _experts_kernel(be_ref, nu_ref, x_ref, wg_ref, wu_ref, wd_ref, bg_ref, bu_ref, bd_ref, y_ref,
                    wgb_ref, wub_ref, wdb_ref):
    j = pl.program_id(0)
    new_expert = jnp.logical_or(j == 0, be_ref[j] != be_ref[jnp.maximum(j - 1, 0)])

    @pl.when(jnp.logical_and(new_expert, j < nu_ref[0]))
    def _():
        rows = wg_ref.shape[0] // 8
        for src, dst in ((wg_ref, wgb_ref), (wu_ref, wub_ref), (wd_ref, wdb_ref)):
            def cast(i, c, src=src, dst=dst):
                r0 = pl.multiple_of(i * rows, rows)
                dst[pl.ds(r0, rows), :] = src[pl.ds(r0, rows), :].astype(BF16)
                return c
            lax.fori_loop(0, 8, cast, 0)

    @pl.when(j < nu_ref[0])
    def _():
        x = x_ref[...].reshape(x_ref.shape[0], wgb_ref.shape[0]).astype(BF16)
        de = wgb_ref.shape[1]
        w = de // EXPERT_SPLIT
        ab = []
        for s in range(EXPERT_SPLIT):
            cols = slice(s * w, (s + 1) * w)
            ab.append((_dot(x, wgb_ref[:, cols]) + bg_ref[:, cols], _dot(x, wub_ref[:, cols]) + bu_ref[:, cols]))
        y = bd_ref[...]
        for s in range(EXPERT_SPLIT):
            a = jnp.minimum(ab[s][0], SWIGLU_LIMIT)
            b = jnp.clip(ab[s][1], -SWIGLU_LIMIT, SWIGLU_LIMIT)
            hid = a * _sigmoid(SWIGLU_ALPHA * a) * (b + 1.0)
            y = y + _dot(hid.astype(BF16), wdb_ref[s * w:(s + 1) * w, :])
        y_ref[...] = y.reshape(y_ref.shape)

    @pl.when(j >= nu_ref[0])
    def _():
        y_ref[...] = jnp.zeros(y_ref.shape, y_ref.dtype)


def _experts(block_e, n_used, xg, wg, wu, wd, bg, bu, bd):
    cap = xg.shape[0]
    d = wg.shape[1]
    row = xg.shape[1:]
    n_blocks = cap // MOE_BLOCK
    de = wg.shape[2]
    wspec = lambda a, b: pl.BlockSpec((None, a, b), lambda j, be, nu: (be[j], 0, 0))
    grid_spec = pltpu.PrefetchScalarGridSpec(
        num_scalar_prefetch=2,
        grid=(n_blocks,),
        in_specs=[pl.BlockSpec((MOE_BLOCK,) + row, lambda j, be, nu: (jnp.minimum(j, nu[0] - 1), 0, 0)),
                  wspec(d, de), wspec(d, de), wspec(de, d),
                  wspec(1, de), wspec(1, de), wspec(1, d)],
        out_specs=pl.BlockSpec((MOE_BLOCK,) + row, lambda j, be, nu: (j, 0, 0)),
        scratch_shapes=[pltpu.VMEM((d, de), BF16), pltpu.VMEM((d, de), BF16), pltpu.VMEM((de, d), BF16)],
    )
    return pl.pallas_call(
        _experts_kernel,
        grid_spec=grid_spec,
        out_shape=jax.ShapeDtypeStruct((cap,) + row, F32),
        compiler_params=pltpu.CompilerParams(dimension_semantics=("arbitrary",),
                                             vmem_limit_bytes=VMEM_LIMIT),
        name="experts",
    )(block_e, n_used, xg, wg, wu, wd, bg, bu, bd)


def _combine_kernel(dcur_ref, dnext_ref, yg_ref, gt_ref, x1_ref, g2_ref, nw_ref, o_ref, ybuf, sems):
    i = pl.program_id(0)
    last = pl.num_programs(0) - 1
    groups = TOK_TILE // ISSUE_UNROLL

    def copy(slot, k, t, d):
        return pltpu.make_async_copy(yg_ref.at[d], ybuf.at[slot, k, t], sems.at[slot])

    def issue(dref, slot):
        def body(g, c):
            for r in range(ISSUE_UNROLL):
                t = g * ISSUE_UNROLL + r
                for k in range(TOP_K):
                    copy(slot, k, t, dref[k, t]).start(priority=k % 2)
            return c

        lax.fori_loop(0, groups, body, 0)

    @pl.when(i == 0)
    def _():
        issue(dcur_ref, 0)

    @pl.when(i < last)
    def _():
        issue(dnext_ref, (i + 1) % 2)

    slot = i % 2

    def drain(g, c):
        for _ in range(ISSUE_UNROLL * TOP_K):
            copy(slot, 0, 0, 0).wait()
        return c

    lax.fori_loop(0, groups, drain, 0)
    d_model = o_ref.shape[-1]
    gt = gt_ref[...]
    f = gt[:, 0:1] * ybuf[slot, 0].reshape(TOK_TILE, d_model)
    for k in range(1, TOP_K):
        f = f + gt[:, k:k + 1] * ybuf[slot, k].reshape(TOK_TILE, d_model)
    ms = jnp.mean(f * f, axis=-1, keepdims=True)
    o_ref[...] = x1_ref[...] + g2_ref[...] * (f * lax.rsqrt(ms + EPS) * nw_ref[...])


def _combine(dest3, yg, gates_tok, x1, mod3, npf2, t_per_batch):
    n, d = x1.shape
    ntile = n // TOK_TILE
    per_b = t_per_batch // TOK_TILE
    return pl.pallas_call(
        _combine_kernel,
        grid=(ntile,),
        in_specs=[pl.BlockSpec((None, TOP_K, TOK_TILE), lambda i: (i, 0, 0), memory_space=pltpu.SMEM),
                  pl.BlockSpec((None, TOP_K, TOK_TILE), lambda i: (jnp.minimum(i + 1, ntile - 1), 0, 0),
                               memory_space=pltpu.SMEM),
                  pl.BlockSpec(memory_space=pl.ANY),
                  pl.BlockSpec((TOK_TILE, LANE), lambda i: (i, 0)),
                  pl.BlockSpec((TOK_TILE, d), lambda i: (i, 0)),
                  pl.BlockSpec((None, 1, d), lambda i: (i // per_b, 0, 5)),
                  pl.BlockSpec((1, d), lambda i: (0, 0))],
        out_specs=pl.BlockSpec((TOK_TILE, d), lambda i: (i, 0)),
        out_shape=jax.ShapeDtypeStruct((n, d), F32),
        scratch_shapes=[pltpu.VMEM((2, TOP_K, TOK_TILE) + yg.shape[1:], F32),
                        pltpu.SemaphoreType.DMA((2,))],
        compiler_params=pltpu.CompilerParams(dimension_semantics=("arbitrary",),
                                             vmem_limit_bytes=VMEM_LIMIT),
        name="combine",
    )(dest3, dest3, yg, gates_tok, x1, mod3, npf2)


def _pad_cols(w, cols):
    return jnp.pad(w, ((0, 0), (0, cols - w.shape[1])))


def kernel(x, c, ctx, c_ctx, w_ada, b_ada, norm_pre_mix, norm_post_mix, norm_pre_ffn, norm_post_ffn,
           w_in, gdn_conv_w, gdn_A_log, gdn_dt_bias, gdn_norm_w, ssd_conv_w, ssd_conv_b, ssd_A_log,
           ssd_dt_bias, ssd_D, ssd_norm_w, w_out, w_router, b_router, w_gate, b_gate, w_up, b_up,
           w_down, b_down):
    bn, t, d = x.shape
    n = bn * t
    l = 0
    mod_rows = -(-(bn + 1) // SUBLANE) * SUBLANE
    cc = jnp.zeros((mod_rows, d), F32).at[:bn].set(c).at[bn].set(c_ctx)
    mod3 = _ada(cc, w_ada[l], b_ada[l]).reshape(mod_rows, 1, 6 * d)

    w_in_b = w_in[l].astype(BF16)
    w_gdn = _pad_cols(w_in_b[:, :GDN_COLS], GDN_COLS_PAD)
    w_ssd = _pad_cols(w_in_b[:, GDN_COLS:], SSD_COLS_PAD)
    npm1 = norm_pre_mix[l].reshape(1, d)
    ug = _inproj(x, mod3, npm1, w_gdn, mod_row=None, col_major=False)
    us = _inproj(x, mod3, npm1, w_ssd, mod_row=None, col_major=True)
    ugc = _inproj(ctx, mod3, npm1, w_gdn, mod_row=bn, col_major=False)
    usc = _inproj(ctx, mod3, npm1, w_ssd, mod_row=bn, col_major=False)

    gdn_o = _gdn(ug, ugc, gdn_conv_w[l], _gdn_gate_params(gdn_A_log[l], gdn_dt_bias[l]),
                 jnp.tile(gdn_norm_w[l], 2).reshape(1, LANE))
    ssd_o = _ssd(us, usc, ssd_conv_w[l], ssd_conv_b[l].reshape(1, SSD_CONV_CH),
                 _ssd_gate_params(ssd_A_log[l], ssd_dt_bias[l]),
                 jnp.repeat(ssd_D[l], SSD_HEADDIM).reshape(SSD_GROUPS, 1, XW),
                 ssd_norm_w[l].reshape(SSD_GROUPS, 1, XW))

    x1, h2, e_t, gates_tok = _mix(
        gdn_o, ssd_o, x, mod3, norm_post_mix[l].reshape(1, d), norm_pre_ffn[l].reshape(1, d),
        w_out[l].astype(BF16), _pad_cols(w_router[l], LANE), _pad_cols(b_router[l].reshape(1, N_EXPERTS), LANE))

    n_blocks = -(-(n * TOP_K) // MOE_BLOCK) + N_EXPERTS
    cap = n_blocks * MOE_BLOCK
    ntile = n // TOK_TILE
    ids2 = e_t.reshape(SUBLANE * ntile, LANE)[:TOP_K * ntile]
    dest, meta = _rank(ids2, n_blocks)
    dest3 = dest.reshape(TOP_K, ntile, LANE).transpose(1, 0, 2)
    block_e = meta[:SUBLANE].reshape(-1)[:n_blocks]
    n_used = meta[SUBLANE, :1]

    xg = _dispatch(meta, dest3, h2, cap)
    yg = _experts(block_e, n_used, xg,
                  w_gate[l], w_up[l], w_down[l],
                  b_gate[l].reshape(N_EXPERTS, 1, D_EXPERT), b_up[l].reshape(N_EXPERTS, 1, D_EXPERT),
                  b_down[l].reshape(N_EXPERTS, 1, d))
    out = _combine(dest3, yg, gates_tok.reshape(n, LANE), x1.reshape(n, d), mod3,
                   norm_post_ffn[l].reshape(1, d), t)
    return out.reshape(bn, t, d)
```

```python
import functools

import jax
import jax.numpy as jnp
from jax import lax
from jax.experimental import pallas as pl
from jax.experimental.pallas import tpu as pltpu

F32 = jnp.float32
BF16 = jnp.bfloat16
I32 = jnp.int32

D_MODEL = 1024
GRID_W = 64
GDN_HEADS = 8
GDN_DK = 64
GDN_DV = 64
SSD_HEADS = 8
SSD_HEADDIM = 64
SSD_GROUPS = 2
SSD_STATE = 128
CONV_W = 5
CHUNK = 64
N_EXPERTS = 32
TOP_K = 4
D_EXPERT = 1024
SWIGLU_LIMIT = 7.0
SWIGLU_ALPHA = 1.702
MOE_BLOCK = 256
EPS = 1e-6

GDN_QK = GDN_HEADS * GDN_DK
GDN_V = GDN_HEADS * GDN_DV
GDN_CONV_CH = 2 * GDN_QK + GDN_V
GDN_COLS = GDN_CONV_CH + GDN_V + 4 * GDN_HEADS
SSD_INNER = SSD_HEADS * SSD_HEADDIM
SSD_BC = SSD_GROUPS * SSD_STATE
SSD_CONV_CH = SSD_INNER + 2 * SSD_BC
SSD_COLS = SSD_CONV_CH + SSD_INNER + 2 * SSD_HEADS

LANE = 128
SUBLANE = 8
GDN_COLS_PAD = 17 * LANE
SSD_COLS_PAD = 13 * LANE
VMEM_LIMIT = 56 * 1024 * 1024
NEG = -1e30
ROW_BLOCK = 256
INTRA_GROUP = 8
SSD_GROUP = 2


def _dot(a, b):
    return jnp.dot(a, b, preferred_element_type=F32)


def _dot_nt(a, b):
    return lax.dot_general(a, b, (((1,), (1,)), ((), ())), preferred_element_type=F32)


def _dot_f32(a, b):
    return jnp.dot(a, b, preferred_element_type=F32, precision=lax.Precision.HIGHEST)


def _silu(x):
    return x * (1.0 / (1.0 + jnp.exp(-x)))


def _sigmoid(x):
    return 1.0 / (1.0 + jnp.exp(-x))


def _softplus(x):
    return jnp.maximum(x, 0.0) + jnp.log(1.0 + jnp.exp(-jnp.abs(x)))


def _iota(shape, axis):
    return lax.broadcasted_iota(I32, shape, axis)


def _split_dot(x, ones_b16):
    hi = x.astype(BF16)
    lo = (x - hi.astype(F32)).astype(BF16)
    return _dot(hi, ones_b16) + _dot(lo, ones_b16)


def _ada_kernel(c_ref, w_ref, b_ref, o_ref):
    o_ref[...] = _dot_f32(_silu(c_ref[...]), w_ref[...]) + b_ref[...]


def _ada(cc, w_ada, b_ada):
    rows, d = cc.shape
    n = w_ada.shape[1]
    tn = 1024
    return pl.pallas_call(
        _ada_kernel,
        grid=(n // tn,),
        in_specs=[pl.BlockSpec((rows, d), lambda j: (0, 0)),
                  pl.BlockSpec((d, tn), lambda j: (0, j)),
                  pl.BlockSpec((1, tn), lambda j: (0, j))],
        out_specs=pl.BlockSpec((rows, tn), lambda j: (0, j)),
        out_shape=jax.ShapeDtypeStruct((rows, n), F32),
        compiler_params=pltpu.CompilerParams(dimension_semantics=("arbitrary",),
                                             vmem_limit_bytes=VMEM_LIMIT),
        name="ada",
    )(cc, w_ada, b_ada.reshape(1, n))


def _modulated_norm(x, nw, sh, sc):
    ms = jnp.mean(x * x, axis=-1, keepdims=True)
    return (x * lax.rsqrt(ms + EPS) * nw) * (1.0 + sc) + sh


def _inproj_kernel(x_ref, sh_ref, sc_ref, nw_ref, w_ref, o_ref, *, pieces):
    if pieces:
        xt = x_ref[...]
        x = jnp.concatenate([xt[:, i * D_MODEL:(i + 1) * D_MODEL] for i in range(pieces)], axis=0)
    else:
        x = x_ref[...]
    h = _modulated_norm(x, nw_ref[...], sh_ref[...], sc_ref[...])
    o_ref[...] = _dot(h.astype(BF16), w_ref[...])


def _inproj(x, mod3, nw, w_b16, *, mod_row, col_major):
    bn, t, d = x.shape
    cols = w_b16.shape[1]
    if col_major:
        rows = t // GRID_W
        pieces = 8
        tm = pieces * rows
        xin = x.reshape(bn, rows, GRID_W * d)
        x_spec = pl.BlockSpec((None, rows, pieces * d), lambda b, i: (b, 0, i))
    else:
        pieces = 0
        tm = min(512, t)
        xin = x
        x_spec = pl.BlockSpec((None, tm, d), lambda b, i: (b, i, 0))
    if mod_row is None:
        row = lambda b: b
    else:
        row = lambda b: mod_row
    return pl.pallas_call(
        functools.partial(_inproj_kernel, pieces=pieces),
        grid=(bn, t // tm),
        in_specs=[x_spec,
                  pl.BlockSpec((None, 1, d), lambda b, i: (row(b), 0, 0)),
                  pl.BlockSpec((None, 1, d), lambda b, i: (row(b), 0, 1)),
                  pl.BlockSpec((1, d), lambda b, i: (0, 0)),
                  pl.BlockSpec((d, cols), lambda b, i: (0, 0))],
        out_specs=pl.BlockSpec((None, tm, cols), lambda b, i: (b, i, 0)),
        out_shape=jax.ShapeDtypeStruct((bn, t, cols), F32),
        compiler_params=pltpu.CompilerParams(dimension_semantics=("arbitrary", "arbitrary"),
                                             vmem_limit_bytes=VMEM_LIMIT),
        name="inproj_cm" if col_major else "inproj",
    )(xin, mod3, mod3, nw, w_b16)


def _conv_silu(src_ref, t, pad_ref, w_ref, bias, dst_ref, post=None):
    _conv_silu_jobs([(src_ref, pad_ref, w_ref, bias, dst_ref, post)], t)


def _conv_silu_jobs(jobs, t):
    width = jobs[0][0].shape[-1]
    zero = jnp.zeros((SUBLANE, width), F32)
    rb = ROW_BLOCK * LANE // width
    nb = t // rb
    for src_ref, pad_ref, _, _, _, _ in jobs:
        pad_ref[0:SUBLANE, :] = zero
        pad_ref[pl.ds(SUBLANE + t, SUBLANE), :] = zero

    def copy(i, c):
        r0 = pl.multiple_of(i * rb, rb)
        for src_ref, pad_ref, _, _, _, _ in jobs:
            pad_ref[pl.ds(SUBLANE + r0, rb), :] = src_ref[pl.ds(r0, rb), :]
        return c

    lax.fori_loop(0, nb, copy, 0)
    ws = [job[2][...] for job in jobs]

    def body(i, c):
        r0 = pl.multiple_of(i * rb, rb)
        for (_, pad_ref, _, bias, dst_ref, post), w in zip(jobs, ws):
            win = pad_ref[pl.ds(r0, rb + 2 * SUBLANE), :]
            acc = None
            for j in range(CONV_W):
                off = SUBLANE - CONV_W // 2 + j
                tap = pltpu.roll(win, rb + 2 * SUBLANE - off, 0)[0:rb, :] * w[j:j + 1, :]
                acc = tap if acc is None else acc + tap
            if bias is not None:
                acc = acc + bias
            y = _silu(acc)
            if post is not None:
                y = post(y)
            dst_ref[pl.ds(r0, rb), :] = y
        return c

    lax.fori_loop(0, nb, body, 0)


def _chunk_cumsums(x, rows_in_chunk):
    n = x.shape[0]
    pre = x
    suf = x
    s = 1
    while s < CHUNK:
        pre = pre + jnp.where(rows_in_chunk >= s, pltpu.roll(pre, s, 0), 0.0)
        suf = suf + jnp.where(rows_in_chunk < CHUNK - s, pltpu.roll(suf, n - s, 0), 0.0)
        s *= 2
    return pre, suf


def _stack_masked(x, groups):
    r, n = x.shape
    w = n // groups
    lane_group = _iota((r, n), 1) // w
    return jnp.concatenate([jnp.where(lane_group == g, x, 0.0) for g in range(groups)], axis=0)


def _gdn_kernel(q_ref, k_ref, v_ref, z_ref, g_ref, kc_ref, vc_ref, gc_ref,
                wq_ref, wk_ref, wv_ref, gpar_ref, nw_ref, o_ref,
                pad_ref, pad2_ref, pad3_ref, qn_ref, kn_ref, vv_ref, gb_ref, gall_ref, gallc_ref,
                u_ref, w_ref, at_ref, qe_ref,
                m_ref, c_ref, ss_ref, egl_ref, oacc_ref, s_ref, *, t_lat, t_ctx):
    hp = pl.program_id(1)
    ones_bd = (_iota((LANE, LANE), 0) // GDN_DK == _iota((LANE, LANE), 1) // GDN_DK).astype(BF16)
    lane64 = _iota((CHUNK, LANE), 1)
    first = lane64 < GDN_DK
    ii = _iota((CHUNK, LANE), 0)
    jj = lane64 % CHUNK
    eye = (ii == jj).astype(F32)
    bd_mask = (_iota((LANE, LANE), 0) // GDN_DK) == (_iota((LANE, LANE), 1) // GDN_DV)
    gpar = gpar_ref[...]
    a_row = jnp.exp(gpar[0:1, :])
    dtb_row = gpar[1:2, :]

    def l2norm(scale):
        def f(y):
            ss = _split_dot(y * y, ones_bd)
            return y * lax.rsqrt(ss + EPS) * scale
        return f

    def gate_all_heads(src_ref, dst_ref, t):
        lane = _iota((ROW_BLOCK, LANE), 1)
        ric = _iota((ROW_BLOCK, LANE), 0) % CHUNK

        def body(i, c):
            r0 = pl.multiple_of(i * ROW_BLOCK, ROW_BLOCK)
            x = src_ref[pl.ds(r0, ROW_BLOCK), :]
            act = jnp.where(lane < 2 * GDN_HEADS, _sigmoid(x), -a_row * _softplus(x + dtb_row))
            pre, suf = _chunk_cumsums(act, ric)
            dst_ref[pl.ds(r0, ROW_BLOCK), :] = jnp.where(lane < 2 * GDN_HEADS, act,
                                                         jnp.where(lane < 3 * GDN_HEADS, pre, suf))
            return c

        lax.fori_loop(0, t // ROW_BLOCK, body, 0)

    @pl.when(hp == 0)
    def _():
        gate_all_heads(gc_ref, gallc_ref, t_ctx)
        gate_all_heads(g_ref, gall_ref, t_lat)

    def gate_block(src_ref, t):
        shift = (LANE - 2 * hp) % LANE

        def body(i, c):
            r0 = pl.multiple_of(i * ROW_BLOCK, ROW_BLOCK)
            gb_ref[pl.ds(r0, ROW_BLOCK), :] = pltpu.roll(src_ref[pl.ds(r0, ROW_BLOCK), :], shift, 1)
            return c

        lax.fori_loop(0, t // ROW_BLOCK, body, 0)

    def expand(tile, c0):
        return jnp.where(first, tile[:, c0:c0 + 1], tile[:, c0 + 1:c0 + 2])

    first2 = jnp.concatenate([first, first], axis=1)
    dirs = (0, 1)

    def intra(ns, with_out):
        r0s = [pl.multiple_of(n * CHUNK, CHUNK) for n in ns]
        ks = [kn_ref[pl.ds(r0, CHUNK), :] for r0 in r0s]
        gs = [gb_ref[pl.ds(r0, CHUNK), :] for r0 in r0s]
        gts = [jnp.concatenate([g, g], axis=0).T for g in gs]
        k_bds = [_stack_masked(k, 2).astype(BF16) for k in ks]
        probs = [(c, d) for c in range(len(ns)) for d in dirs]
        beta, gx, rel, gl, kb = {}, {}, {}, {}, {}
        for c, d in probs:
            g = gs[c]
            c0 = 2 * GDN_HEADS + GDN_HEADS * d
            beta[c, d] = expand(g, GDN_HEADS * d)
            gx[c, d] = expand(g, c0)
            grow = jnp.where(first[0:1, :], gts[c][c0:c0 + 1, :], gts[c][c0 + 1:c0 + 2, :])
            rel[c, d] = gx[c, d] - grow
            gl[c, d] = gx[c, d][CHUNK - 1:CHUNK, :] if d == 0 else gx[c, d][0:1, :]
            kb[c, d] = ks[c] * beta[c, d]
        strict = {0: ii > jj, 1: ii < jj}
        incl = {0: ii >= jj, 1: ii <= jj}
        lhs = []
        for c in range(len(ns)):
            parts = [kb[c, 0], kb[c, 1]]
            if with_out:
                parts.append(qn_ref[pl.ds(r0s[c], CHUNK), :])
            lhs.append(jnp.concatenate(parts, axis=0).astype(BF16))
        kk = [_dot_nt(lhs[c], k_bds[c]) for c in range(len(ns))]
        nm, p = {}, {}
        for c, d in probs:
            a = kk[c][d * CHUNK:(d + 1) * CHUNK, :] * jnp.exp(jnp.where(strict[d], rel[c, d], NEG))
            nm[c, d] = -a
            p[c, d] = eye + nm[c, d]
        for key in probs:
            nm[key] = _dot(nm[key].astype(BF16), _stack_masked(nm[key], 2).astype(BF16))
        for _ in range(4):
            for key in probs:
                both = _dot(jnp.concatenate([nm[key], p[key]], axis=0).astype(BF16),
                            _stack_masked(nm[key], 2).astype(BF16))
                nm[key] = both[0:CHUNK, :]
                p[key] = p[key] + both[CHUNK:2 * CHUNK, :]
        for key in probs:
            p[key] = p[key] + _dot(p[key].astype(BF16), _stack_masked(nm[key], 2).astype(BF16))
        vs = [vv_ref[pl.ds(r0, CHUNK), :] for r0 in r0s]
        eg, uw = {}, {}
        for c, d in probs:
            eg[c, d] = jnp.exp(gx[c, d])
            rhs = jnp.concatenate([vs[c] * beta[c, d], kb[c, d] * eg[c, d]], axis=1)
            rhs_bd = jnp.concatenate([jnp.where(first2, rhs, 0.0), jnp.where(first2, 0.0, rhs)], axis=0)
            uw[c, d] = _dot(p[c, d].astype(BF16), rhs_bd.astype(BF16))
        mc = {}
        for c, d in probs:
            kdec = ks[c] * jnp.exp(gl[c, d] - gx[c, d])
            kdt = jnp.concatenate([kdec, jnp.zeros_like(kdec)], axis=0).T
            wu = jnp.concatenate([uw[c, d][:, LANE:], uw[c, d][:, :LANE]], axis=1).astype(BF16)
            mc[c, d] = _dot(kdt.astype(BF16), jnp.concatenate([wu, jnp.zeros_like(wu)], axis=0))
        for c, d in probs:
            r0, n = r0s[c], ns[c]
            m0 = pl.multiple_of(n * LANE, LANE)
            m_ref[d, pl.ds(m0, LANE), :] = jnp.where(bd_mask, mc[c, d][:, :LANE], 0.0).astype(BF16)
            c_ref[d, pl.ds(m0, LANE), :] = jnp.where(bd_mask, mc[c, d][:, LANE:], 0.0)
            egl_ref[d, n] = jnp.broadcast_to(jnp.exp(gl[c, d]), (SUBLANE, LANE))
            if with_out:
                u_ref[d, pl.ds(r0, CHUNK), :] = uw[c, d][:, :LANE]
                w_ref[d, pl.ds(r0, CHUNK), :] = uw[c, d][:, LANE:].astype(BF16)
                at = kk[c][2 * CHUNK:3 * CHUNK, :] * jnp.exp(jnp.where(incl[d], rel[c, d], NEG))
                at_ref[d, pl.ds(r0, CHUNK), :] = at.astype(BF16)
                qe_ref[d, pl.ds(r0, CHUNK), :] = (qn_ref[pl.ds(r0, CHUNK), :] * eg[c, d]).astype(BF16)

    def scan_step(ns, with_out):
        m0s = [pl.multiple_of(n * LANE, LANE) for n in ns]
        s = [s_ref[d] for d in dirs]
        sb = [s[d].astype(BF16) for d in dirs]
        ms = [_dot(m_ref[d, pl.ds(m0s[d], LANE), :], sb[d]) for d in dirs]
        for d in dirs:
            s_ref[d] = s[d] * egl_ref[d, ns[d]][0:1, :] - ms[d] + c_ref[d, pl.ds(m0s[d], LANE), :]
            if with_out:
                ss_ref[d, pl.ds(m0s[d], LANE), :] = sb[d]

    def outputs(ns):
        cs = range(len(ns))
        probs = [(c, d) for c in cs for d in dirs]
        r0 = [pl.multiple_of(n * CHUNK, CHUNK) for n in ns]
        m0 = [pl.multiple_of(n * LANE, LANE) for n in ns]
        ws = {(c, d): _dot(jnp.concatenate([w_ref[d, pl.ds(r0[c], CHUNK), :], qe_ref[d, pl.ds(r0[c], CHUNK), :]],
                                           axis=0), ss_ref[d, pl.ds(m0[c], LANE), :]) for c, d in probs}
        vnew = {(c, d): u_ref[d, pl.ds(r0[c], CHUNK), :] - ws[c, d][0:CHUNK, :] for c, d in probs}
        av = {(c, d): _dot(at_ref[d, pl.ds(r0[c], CHUNK), :], _stack_masked(vnew[c, d], 2).astype(BF16))
              for c, d in probs}
        for c in cs:
            oacc_ref[pl.ds(r0[c], CHUNK), :] = (ws[c, 0][CHUNK:2 * CHUNK, :] + av[c, 0]
                                                + ws[c, 1][CHUNK:2 * CHUNK, :] + av[c, 1])

    def phase(t, k_src, v_src, g_src, q_src, with_out):
        nc = t // CHUNK
        jobs = [(k_src, pad_ref, wk_ref, None, kn_ref, l2norm(1.0)),
                (v_src, pad2_ref, wv_ref, None, vv_ref, None)]
        if with_out:
            jobs.append((q_src, pad3_ref, wq_ref, None, qn_ref, l2norm(GDN_DK ** -0.5)))
        _conv_silu_jobs(jobs, t)
        gate_block(g_src, t)

        group = min(INTRA_GROUP, nc)

        def intra_body(i, c):
            intra([i * group + j for j in range(group)], with_out)
            return c

        lax.fori_loop(0, nc // group, intra_body, 0)

        def scan_body(i, c):
            scan_step([i, nc - 1 - i], with_out)
            return c

        lax.fori_loop(0, nc, scan_body, 0)
        if with_out:
            def out_group(i, c):
                outputs([i * group + j for j in range(group)])
                return c

            lax.fori_loop(0, nc // group, out_group, 0)

    s_ref[...] = jnp.zeros(s_ref.shape, F32)
    phase(t_ctx, kc_ref, vc_ref, gallc_ref, None, False)
    phase(t_lat, k_ref, v_ref, gall_ref, q_ref, True)

    nw = nw_ref[...]

    def out_body(i, c):
        r0 = pl.multiple_of(i * ROW_BLOCK, ROW_BLOCK)
        o = oacc_ref[pl.ds(r0, ROW_BLOCK), :]
        ms = _split_dot(o * o, ones_bd) * (1.0 / GDN_DV)
        y = o * lax.rsqrt(ms + EPS) * nw * _silu(z_ref[pl.ds(r0, ROW_BLOCK), :])
        o_ref[pl.ds(r0, ROW_BLOCK), :] = y.astype(o_ref.dtype)
        return c

    lax.fori_loop(0, t_lat // ROW_BLOCK, out_body, 0)


def _gdn(u_lat, u_ctx, conv_w, gpar, nw2):
    bn, t, _ = u_lat.shape
    tc = u_ctx.shape[1]
    npairs = GDN_HEADS // 2
    nc = t // CHUNK
    col = lambda off: (lambda b, h: (b, 0, off + h))
    lat = lambda off: pl.BlockSpec((None, t, LANE), col(off))
    ctx = lambda off: pl.BlockSpec((None, tc, LANE), col(off))
    gate_blk = GDN_CONV_CH // LANE + GDN_V // LANE
    fixed = lambda b, h: (b, 0, gate_blk)
    cw = lambda off: pl.BlockSpec((CONV_W, LANE), lambda b, h: (0, off + h))
    return pl.pallas_call(
        functools.partial(_gdn_kernel, t_lat=t, t_ctx=tc),
        grid=(bn, npairs),
        in_specs=[lat(0), lat(npairs), lat(2 * npairs), lat(3 * npairs),
                  pl.BlockSpec((None, t, LANE), fixed),
                  ctx(npairs), ctx(2 * npairs), pl.BlockSpec((None, tc, LANE), fixed),
                  cw(0), cw(npairs), cw(2 * npairs),
                  pl.BlockSpec((SUBLANE, LANE), lambda b, h: (0, 0)),
                  pl.BlockSpec((1, LANE), lambda b, h: (0, 0))],
        out_specs=pl.BlockSpec((None, t, LANE), lambda b, h: (b, 0, h)),
        out_shape=jax.ShapeDtypeStruct((bn, t, GDN_V), BF16),
        scratch_shapes=[pltpu.VMEM((t + 2 * SUBLANE, LANE), F32),
                        pltpu.VMEM((t + 2 * SUBLANE, LANE), F32),
                        pltpu.VMEM((t + 2 * SUBLANE, LANE), F32),
                        pltpu.VMEM((t, LANE), F32),
                        pltpu.VMEM((t, LANE), F32),
                        pltpu.VMEM((t, LANE), F32),
                        pltpu.VMEM((t, LANE), F32),
                        pltpu.VMEM((t, LANE), F32),
                        pltpu.VMEM((tc, LANE), F32),
                        pltpu.VMEM((2, t, LANE), F32),
                        pltpu.VMEM((2, t, LANE), BF16),
                        pltpu.VMEM((2, t, LANE), BF16),
                        pltpu.VMEM((2, t, LANE), BF16),
                        pltpu.VMEM((2, nc * LANE, LANE), BF16),
                        pltpu.VMEM((2, nc * LANE, LANE), F32),
                        pltpu.VMEM((2, nc * LANE, LANE), BF16),
                        pltpu.VMEM((2, nc, SUBLANE, LANE), F32),
                        pltpu.VMEM((t, LANE), F32),
                        pltpu.VMEM((2, LANE, LANE), F32)],
        compiler_params=pltpu.CompilerParams(dimension_semantics=("arbitrary", "arbitrary"),
                                             vmem_limit_bytes=VMEM_LIMIT),
        name="gdn",
    )(u_lat, u_lat, u_lat, u_lat, u_lat, u_ctx, u_ctx, u_ctx, conv_w, conv_w, conv_w, gpar, nw2)


HG = SSD_HEADS // SSD_GROUPS
XW = HG * SSD_HEADDIM


def _ssd_kernel(x_ref, b_ref, c_ref, z_ref, g_ref, xc_ref, bc_ref, gc_ref,
                wx_ref, wb_ref, wc_ref, bx_ref, bb_ref, bcb_ref, gpar_ref, dsk_ref, nw_ref, o_ref,
                padx_ref, padb_ref, xs_ref, bs_ref, cs_ref, gb_ref, yacc_ref, s_ref, *, t_lat, t_ctx):
    grp = pl.program_id(1)
    lane4 = _iota((CHUNK, XW), 1)
    lg = lane4 // SSD_HEADDIM
    ii = _iota((CHUNK, XW), 0)
    jj = lane4 % CHUNK
    gpar = gpar_ref[...]
    a_row = -jnp.exp(gpar[0:1, :])
    dtb_row = gpar[1:2, :]

    def gate_block(src_ref, t):
        lane = _iota((ROW_BLOCK, LANE), 1)
        ric = _iota((ROW_BLOCK, LANE), 0) % CHUNK
        shift = (LANE - HG * grp) % LANE

        def body(i, c):
            r0 = pl.multiple_of(i * ROW_BLOCK, ROW_BLOCK)
            x = src_ref[pl.ds(r0, ROW_BLOCK), :]
            dt = _softplus(x + dtb_row)
            a = pltpu.roll(dt * a_row, 2 * SSD_HEADS, 1)
            pre, suf = _chunk_cumsums(a, ric)
            blk = jnp.where(lane < 2 * SSD_HEADS, dt, jnp.where(lane < 3 * SSD_HEADS, pre, suf))
            gb_ref[pl.ds(r0, ROW_BLOCK), :] = pltpu.roll(blk, shift, 1)
            return c

        lax.fori_loop(0, t // ROW_BLOCK, body, 0)

    def expand(tile, c0):
        out = tile[:, c0 + HG - 1:c0 + HG]
        for h in range(HG - 2, -1, -1):
            out = jnp.where(lg == h, tile[:, c0 + h:c0 + h + 1], out)
        return out

    def rowsel(gt2, c0):
        out = gt2[c0 + HG - 1:c0 + HG, :]
        for h in range(HG - 2, -1, -1):
            out = jnp.where(lg[0:1, :] == h, gt2[c0 + h:c0 + h + 1, :], out)
        return out

    def step(chunks, with_out):
        probs = [(d, j) for d in (0, 1) for j in range(len(chunks[d]))]
        r0, ax, al, x, bm, cm, xw, upd, yin = {}, {}, {}, {}, {}, {}, {}, {}, {}
        for key in probs:
            d, j = key
            r0[key] = pl.multiple_of(chunks[d][j] * CHUNK, CHUNK)
            g = gb_ref[pl.ds(r0[key], CHUNK), :]
            dtx = expand(g, SSD_HEADS * d)
            ax[key] = expand(g, 2 * SSD_HEADS + SSD_HEADS * d)
            al[key] = ax[key][CHUNK - 1:CHUNK, :] if d == 0 else ax[key][0:1, :]
            x[key] = xs_ref[pl.ds(r0[key], CHUNK), :]
            bm[key] = bs_ref[pl.ds(r0[key], CHUNK), :]
            xw[key] = (x[key] * (jnp.exp(al[key] - ax[key]) * dtx)).astype(BF16)
            if with_out:
                gt = jnp.concatenate([g, g], axis=0).T
                gt2 = jnp.concatenate([gt, gt], axis=1)
                arow = rowsel(gt2, 2 * SSD_HEADS + SSD_HEADS * d)
                dtrow = rowsel(gt2, SSD_HEADS * d)
                incl = (ii >= jj) if d == 0 else (ii <= jj)
                yin[key] = jnp.exp(jnp.where(incl, ax[key] - arow, NEG)) * dtrow
                cm[key] = cs_ref[pl.ds(r0[key], CHUNK), :].astype(BF16)
        if with_out:
            cb = {key: _dot_nt(cm[key], jnp.concatenate([bm[key]] * HG, axis=0).astype(BF16)) for key in probs}
        for key in probs:
            bt = jnp.concatenate([bm[key], jnp.zeros_like(bm[key])], axis=0).T.astype(BF16)
            upd[key] = _dot(bt, jnp.concatenate([xw[key], jnp.zeros_like(xw[key])], axis=0))
        if with_out:
            for key in probs:
                yin[key] = _dot((cb[key] * yin[key]).astype(BF16), _stack_masked(x[key], HG).astype(BF16))
        for d in (0, 1):
            s = s_ref[d]
            for j in range(len(chunks[d])):
                key = (d, j)
                if with_out:
                    y = yin[key] + _dot(cm[key], s.astype(BF16)) * jnp.exp(ax[key])
                    yacc_ref[pl.ds(r0[key], CHUNK), :] += y
                s = s * jnp.exp(al[key]) + upd[key]
            s_ref[d] = s

    def phase(t, x_src, b_src, c_src, g_src, with_out):
        nc = t // CHUNK
        _conv_silu(x_src, t, padx_ref, wx_ref, bx_ref[...], xs_ref)
        _conv_silu(b_src, t, padb_ref, wb_ref, bb_ref[...], bs_ref)
        if with_out:
            _conv_silu(c_src, t, padb_ref, wc_ref, bcb_ref[...], cs_ref)
        gate_block(g_src, t)

        def body(i, c):
            fwd = [i * SSD_GROUP + j for j in range(SSD_GROUP)]
            bwd = [nc - 1 - i * SSD_GROUP - j for j in range(SSD_GROUP)]
            step([fwd, bwd], with_out)
            return c

        lax.fori_loop(0, nc // SSD_GROUP, body, 0)

    s_ref[...] = jnp.zeros(s_ref.shape, F32)
    yacc_ref[...] = jnp.zeros(yacc_ref.shape, F32)
    phase(t_ctx, xc_ref, bc_ref, None, gc_ref, False)
    phase(t_lat, x_ref, b_ref, c_ref, g_ref, True)

    dsk = dsk_ref[...]
    nw = nw_ref[...]

    rb = ROW_BLOCK * LANE // XW

    def out_body(i, c):
        r0 = pl.multiple_of(i * rb, rb)
        y = yacc_ref[pl.ds(r0, rb), :] + dsk * xs_ref[pl.ds(r0, rb), :]
        y = y * _silu(z_ref[pl.ds(r0, rb), :])
        ms = jnp.mean(y * y, axis=-1, keepdims=True)
        o_ref[pl.ds(r0, rb), :] = (y * lax.rsqrt(ms + EPS) * nw).astype(o_ref.dtype)
        return c

    lax.fori_loop(0, t_lat // rb, out_body, 0)


def _ssd(u_lat, u_ctx, conv_w, conv_b, gpar, dskip, nw):
    bn, t, _ = u_lat.shape
    tc = u_ctx.shape[1]
    x_blk = lambda tt: pl.BlockSpec((None, tt, XW), lambda b, g: (b, 0, g))
    n_blk = lambda tt, off: pl.BlockSpec((None, tt, LANE), lambda b, g: (b, 0, off + g))
    b_off = SSD_INNER // LANE
    c_off = b_off + SSD_BC // LANE
    z_off = SSD_CONV_CH // XW
    dt_blk = (SSD_CONV_CH + SSD_INNER) // LANE
    fixed = lambda b, g: (b, 0, dt_blk)
    return pl.pallas_call(
        functools.partial(_ssd_kernel, t_lat=t, t_ctx=tc),
        grid=(bn, SSD_GROUPS),
        in_specs=[x_blk(t), n_blk(t, b_off), n_blk(t, c_off),
                  pl.BlockSpec((None, t, XW), lambda b, g: (b, 0, z_off + g)),
                  pl.BlockSpec((None, t, LANE), fixed),
                  x_blk(tc), n_blk(tc, b_off), pl.BlockSpec((None, tc, LANE), fixed),
                  pl.BlockSpec((CONV_W, XW), lambda b, g: (0, g)),
                  pl.BlockSpec((CONV_W, LANE), lambda b, g: (0, b_off + g)),
                  pl.BlockSpec((CONV_W, LANE), lambda b, g: (0, c_off + g)),
                  pl.BlockSpec((1, XW), lambda b, g: (0, g)),
                  pl.BlockSpec((1, LANE), lambda b, g: (0, b_off + g)),
                  pl.BlockSpec((1, LANE), lambda b, g: (0, c_off + g)),
                  pl.BlockSpec((SUBLANE, LANE), lambda b, g: (0, 0)),
                  pl.BlockSpec((None, 1, XW), lambda b, g: (g, 0, 0)),
                  pl.BlockSpec((None, 1, XW), lambda b, g: (g, 0, 0))],
        out_specs=pl.BlockSpec((None, t, XW), lambda b, g: (b, 0, g)),
        out_shape=jax.ShapeDtypeStruct((bn, t, SSD_INNER), BF16),
        scratch_shapes=[pltpu.VMEM((t + 2 * SUBLANE, XW), F32),
                        pltpu.VMEM((t + 2 * SUBLANE, LANE), F32),
                        pltpu.VMEM((t, XW), F32),
                        pltpu.VMEM((t, LANE), F32),
                        pltpu.VMEM((t, LANE), F32),
                        pltpu.VMEM((t, LANE), F32),
                        pltpu.VMEM((t, XW), F32),
                        pltpu.VMEM((2, SSD_STATE, XW), F32)],
        compiler_params=pltpu.CompilerParams(dimension_semantics=("arbitrary", "arbitrary"),
                                             vmem_limit_bytes=VMEM_LIMIT),
        name="ssd",
    )(u_lat, u_lat, u_lat, u_lat, u_lat, u_ctx, u_ctx, u_ctx,
      conv_w, conv_w, conv_w, conv_b, conv_b, conv_b, gpar, dskip, nw)


def _gdn_gate_params(a_log, dt_bias):
    z = jnp.zeros((SUBLANE, LANE), F32)
    z = z.at[0, 2 * GDN_HEADS:4 * GDN_HEADS].set(a_log.reshape(-1))
    return z.at[1, 2 * GDN_HEADS:4 * GDN_HEADS].set(dt_bias.reshape(-1))


def _ssd_gate_params(a_log, dt_bias):
    z = jnp.zeros((SUBLANE, LANE), F32)
    z = z.at[0, 0:2 * SSD_HEADS].set(a_log.reshape(-1))
    return z.at[1, 0:2 * SSD_HEADS].set(dt_bias.reshape(-1))


MIX_TM = 256
TOK_TILE = 128


def _mix_kernel(go_ref, so_ref, x_ref, g1_ref, sh2_ref, sc2_ref, npm_ref, npf_ref, wo_ref, wr_ref, br_ref,
                x1_ref, h2_ref, et_ref, gt_ref):
    so = so_ref[...]
    rows = MIX_TM // GRID_W
    ssd = jnp.concatenate([so[:, r * SSD_INNER:(r + 1) * SSD_INNER] for r in range(rows)], axis=0)
    m = _dot(go_ref[...], wo_ref[0:GDN_V, :]) + _dot(ssd, wo_ref[GDN_V:GDN_V + SSD_INNER, :])
    ms = jnp.mean(m * m, axis=-1, keepdims=True)
    x1 = x_ref[...] + g1_ref[...] * (m * lax.rsqrt(ms + EPS) * npm_ref[...])
    x1_ref[...] = x1
    h2 = _modulated_norm(x1, npf_ref[...], sh2_ref[...], sc2_ref[...])
    h2_ref[...] = h2.reshape(h2_ref.shape)
    wr = wr_ref[...]
    h_hi = h2.astype(BF16)
    h_lo = (h2 - h_hi.astype(F32)).astype(BF16)
    w_hi = wr.astype(BF16)
    w_lo = (wr - w_hi.astype(F32)).astype(BF16)
    logits = _dot(h_hi, w_hi) + (_dot(h_hi, w_lo) + _dot(h_lo, w_hi)) + br_ref[...]
    lane = _iota(logits.shape, 1)
    lane_f = lane.astype(F32)
    l = jnp.where(lane < N_EXPERTS, logits, NEG)
    vals = []
    for _ in range(TOP_K):
        mk = jnp.max(l, axis=1, keepdims=True)
        idx = jnp.min(jnp.where(l == mk, lane_f, float(LANE)), axis=1, keepdims=True)
        l = jnp.where(lane_f == idx, NEG, l)
        vals.append(mk)
    ex = [jnp.exp(v - vals[0]) for v in vals]
    den = ex[0] + ex[1] + ex[2] + ex[3]
    gates = jnp.zeros(logits.shape, F32)
    for k in range(TOP_K):
        gates = jnp.where(lane == k, ex[k] / den, gates)
    gt_ref[...] = gates
    lt = logits.T[0:N_EXPERTS, :]
    row_f = _iota(lt.shape, 0).astype(F32)
    ids = []
    for _ in range(TOP_K):
        mk = jnp.max(lt, axis=0, keepdims=True)
        idx = jnp.min(jnp.where(lt == mk, row_f, float(N_EXPERTS)), axis=0, keepdims=True)
        lt = jnp.where(row_f == idx, NEG, lt)
        ids.append(idx.astype(I32))
    pad = jnp.full((SUBLANE - TOP_K, lt.shape[1]), -1, I32)
    et_ref[...] = jnp.concatenate(ids + [pad], axis=0)


def _mix(gdn_o, ssd_o, x, mod3, npm, npf, w_out_b16, w_router_pad, b_router_pad):
    bn, t, d = x.shape
    n = bn * t
    rows = MIX_TM // GRID_W
    nt = t // MIX_TM
    ssd_v = ssd_o.reshape(bn, GRID_W, (t // GRID_W) * SSD_INNER)
    modrow = lambda blk: pl.BlockSpec((None, 1, d), lambda b, i: (b, 0, blk))
    return pl.pallas_call(
        _mix_kernel,
        grid=(bn, nt),
        in_specs=[pl.BlockSpec((None, MIX_TM, GDN_V), lambda b, i: (b, i, 0)),
                  pl.BlockSpec((None, GRID_W, rows * SSD_INNER), lambda b, i: (b, 0, i)),
                  pl.BlockSpec((None, MIX_TM, d), lambda b, i: (b, i, 0)),
                  modrow(2), modrow(3), modrow(4),
                  pl.BlockSpec((1, d), lambda b, i: (0, 0)),
                  pl.BlockSpec((1, d), lambda b, i: (0, 0)),
                  pl.BlockSpec((GDN_V + SSD_INNER, d), lambda b, i: (0, 0)),
                  pl.BlockSpec((d, LANE), lambda b, i: (0, 0)),
                  pl.BlockSpec((1, LANE), lambda b, i: (0, 0))],
        out_specs=[pl.BlockSpec((None, MIX_TM, d), lambda b, i: (b, i, 0)),
                   pl.BlockSpec((MIX_TM, SUBLANE, d // SUBLANE), lambda b, i: (b * nt + i, 0, 0)),
                   pl.BlockSpec((SUBLANE, MIX_TM), lambda b, i: (0, b * nt + i)),
                   pl.BlockSpec((None, MIX_TM, LANE), lambda b, i: (b, i, 0))],
        out_shape=[jax.ShapeDtypeStruct((bn, t, d), F32),
                   jax.ShapeDtypeStruct((n, SUBLANE, d // SUBLANE), F32),
                   jax.ShapeDtypeStruct((SUBLANE, n), I32),
                   jax.ShapeDtypeStruct((bn, t, LANE), F32)],
        compiler_params=pltpu.CompilerParams(dimension_semantics=("arbitrary", "arbitrary"),
                                             vmem_limit_bytes=VMEM_LIMIT),
        name="mix",
    )(gdn_o, ssd_v, x, mod3, mod3, mod3, npm, npf, w_out_b16, w_router_pad, b_router_pad)


def _rank_kernel(ids_ref, dest_ref, meta_ref, tril_ref, cnt_ref, rank_ref, *, n_blocks):
    nrow = ids_ref.shape[0]
    ids = ids_ref[...]
    tril_ref[...] = (_iota((nrow, nrow), 0) > _iota((nrow, nrow), 1)).astype(BF16)
    upper = (_iota((LANE, LANE), 0) < _iota((LANE, LANE), 1)).astype(BF16)
    ones = jnp.ones((LANE, LANE), BF16)
    rank_ref[...] = jnp.zeros(rank_ref.shape, F32)

    def count(e, c):
        hit = ids == e
        mask = hit.astype(F32).astype(BF16)
        before = _dot(mask, upper)
        rowsum = _dot(mask, ones)
        rows_before = _dot(tril_ref[...], rowsum.astype(BF16))
        rank_ref[...] += jnp.where(hit, before + rows_before, 0.0)
        cnt_ref[pl.ds(e, 1), :] = rows_before[nrow - 1:nrow, :] + rowsum[nrow - 1:nrow, :]
        return c

    lax.fori_loop(0, N_EXPERTS, count, 0)
    cnt = cnt_ref[...]
    padded = jnp.floor((cnt + (MOE_BLOCK - 1)) * (1.0 / MOE_BLOCK)) * MOE_BLOCK
    erow = _iota(cnt.shape, 0)
    pad_end = padded
    s = 1
    while s < N_EXPERTS:
        pad_end = pad_end + jnp.where(erow >= s, pltpu.roll(pad_end, s, 0), 0.0)
        s *= 2
    cnt_ref[...] = pad_end - padded
    jblk = (_iota((SUBLANE, LANE), 0) * LANE + _iota((SUBLANE, LANE), 1)).astype(F32) * MOE_BLOCK
    block_e = jnp.zeros((SUBLANE, LANE), F32)
    for e in range(N_EXPERTS):
        block_e = block_e + (pad_end[e:e + 1, :] <= jblk).astype(F32)
    block_e = jnp.minimum(block_e, N_EXPERTS - 1.0)
    n_used = pad_end[N_EXPERTS - 1:N_EXPERTS, :] * (1.0 / MOE_BLOCK)
    on_diag = erow == _iota(cnt.shape, 1)
    pad_end_l = jnp.sum(jnp.where(on_diag, pad_end, 0.0), axis=0, keepdims=True)
    cnt_l = jnp.sum(jnp.where(on_diag, cnt, 0.0), axis=0, keepdims=True)
    r8 = _iota((SUBLANE, LANE), 0)
    tail = jnp.where(r8 == 0, n_used, jnp.where(r8 == 1, pad_end_l, jnp.where(r8 == 2, cnt_l, 0.0)))
    meta_ref[...] = jnp.concatenate([block_e, tail], axis=0).astype(I32)

    def place(e, c):
        rank_ref[...] += jnp.where(ids == e, cnt_ref[pl.ds(e, 1), :], 0.0)
        return c

    lax.fori_loop(0, N_EXPERTS, place, 0)
    dest_ref[...] = rank_ref[...].astype(I32)


def _rank(ids2, n_blocks):
    nrow = ids2.shape[0]
    return pl.pallas_call(
        functools.partial(_rank_kernel, n_blocks=n_blocks),
        in_specs=[pl.BlockSpec((nrow, LANE), lambda: (0, 0))],
        out_specs=[pl.BlockSpec((nrow, LANE), lambda: (0, 0)),
                   pl.BlockSpec((2 * SUBLANE, LANE), lambda: (0, 0))],
        out_shape=[jax.ShapeDtypeStruct((nrow, LANE), I32),
                   jax.ShapeDtypeStruct((2 * SUBLANE, LANE), I32)],
        scratch_shapes=[pltpu.VMEM((nrow, nrow), BF16),
                        pltpu.VMEM((N_EXPERTS, LANE), F32),
                        pltpu.VMEM((nrow, LANE), F32)],
        compiler_params=pltpu.CompilerParams(vmem_limit_bytes=VMEM_LIMIT),
        name="rank",
    )(ids2)


EXPERT_SPLIT = 2


ISSUE_UNROLL = 8


META_N_USED = SUBLANE
META_PAD_END = SUBLANE + 1
META_COUNT = SUBLANE + 2


def _dispatch_kernel(meta_ref, dest_ref, h_ref, xg_ref, zero_ref, sem, zsem):
    n_blocks = xg_ref.shape[0] // MOE_BLOCK

    def zero_copy(start):
        return pltpu.make_async_copy(zero_ref, xg_ref.at[pl.ds(start, MOE_BLOCK)], zsem)

    def zero_fill(action):
        for e in range(N_EXPERTS):
            @pl.when(meta_ref[META_COUNT, e] > 0)
            def _(e=e):
                action(zero_copy(meta_ref[META_PAD_END, e] - MOE_BLOCK))
        for j in range(n_blocks - N_EXPERTS, n_blocks):
            @pl.when(j >= meta_ref[META_N_USED, 0])
            def _(j=j):
                action(zero_copy(j * MOE_BLOCK))

    @pl.when(pl.program_id(0) == 0)
    def _():
        zero_ref[...] = jnp.zeros(zero_ref.shape, zero_ref.dtype)
        zero_fill(lambda cp: cp.start())
        zero_fill(lambda cp: cp.wait())

    def copy(t, d):
        return pltpu.make_async_copy(h_ref.at[t], xg_ref.at[d], sem)

    def issue(i, c):
        for r in range(ISSUE_UNROLL):
            t = i * ISSUE_UNROLL + r
            for k in range(TOP_K):
                copy(t, dest_ref[k, t]).start(priority=k % 2)
        return c

    lax.fori_loop(0, TOK_TILE // ISSUE_UNROLL, issue, 0)

    def drain(i, c):
        for _ in range(ISSUE_UNROLL * TOP_K):
            copy(0, 0).wait()
        return c

    lax.fori_loop(0, TOK_TILE // ISSUE_UNROLL, drain, 0)


def _dispatch(meta, dest3, h3, cap):
    ntile = dest3.shape[0]
    row = h3.shape[1:]
    return pl.pallas_call(
        _dispatch_kernel,
        grid=(ntile,),
        in_specs=[pl.BlockSpec(meta.shape, lambda i: (0, 0), memory_space=pltpu.SMEM),
                  pl.BlockSpec((None, TOP_K, TOK_TILE), lambda i: (i, 0, 0), memory_space=pltpu.SMEM),
                  pl.BlockSpec((TOK_TILE,) + row, lambda i: (i, 0, 0))],
        out_specs=pl.BlockSpec(memory_space=pl.ANY),
        out_shape=jax.ShapeDtypeStruct((cap,) + row, h3.dtype),
        scratch_shapes=[pltpu.VMEM((MOE_BLOCK,) + row, h3.dtype),
                        pltpu.SemaphoreType.DMA(()),
                        pltpu.SemaphoreType.DMA(())],
        compiler_params=pltpu.CompilerParams(dimension_semantics=("arbitrary",)),
        name="dispatch",
    )(meta, dest3, h3)


def _experts_kernel(be_ref, nu_ref, x_ref, wg_ref, wu_ref, wd_ref, bg_ref, bu_ref, bd_ref, y_ref,
                    wgb_ref, wub_ref, wdb_ref):
    j = pl.program_id(0)
    new_expert = jnp.logical_or(j == 0, be_ref[j] != be_ref[jnp.maximum(j - 1, 0)])

    @pl.when(jnp.logical_and(new_expert, j < nu_ref[0]))
    def _():
        rows = wg_ref.shape[0] // 8
        for src, dst in ((wg_ref, wgb_ref), (wu_ref, wub_ref), (wd_ref, wdb_ref)):
            def cast(i, c, src=src, dst=dst):
                r0 = pl.multiple_of(i * rows, rows)
                dst[pl.ds(r0, rows), :] = src[pl.ds(r0, rows), :].astype(BF16)
                return c
            lax.fori_loop(0, 8, cast, 0)

    @pl.when(j < nu_ref[0])
    def _():
        x = x_ref[...].reshape(x_ref.shape[0], wgb_ref.shape[0]).astype(BF16)
        de = wgb_ref.shape[1]
        w = de // EXPERT_SPLIT
        ab = []
        for s in range(EXPERT_SPLIT):
            cols = slice(s * w, (s + 1) * w)
            ab.append((_dot(x, wgb_ref[:, cols]) + bg_ref[:, cols], _dot(x, wub_ref[:, cols]) + bu_ref[:, cols]))
        y = bd_ref[...]
        for s in range(EXPERT_SPLIT):
            a = jnp.minimum(ab[s][0], SWIGLU_LIMIT)
            b = jnp.clip(ab[s][1], -SWIGLU_LIMIT, SWIGLU_LIMIT)
            hid = a * _sigmoid(SWIGLU_ALPHA * a) * (b + 1.0)
            y = y + _dot(hid.astype(BF16), wdb_ref[s * w:(s + 1) * w, :])
        y_ref[...] = y.reshape(y_ref.shape)

    @pl.when(j >= nu_ref[0])
    def _():
        y_ref[...] = jnp.zeros(y_ref.shape, y_ref.dtype)


def _experts(block_e, n_used, xg, wg, wu, wd, bg, bu, bd):
    cap = xg.shape[0]
    d = wg.shape[1]
    row = xg.shape[1:]
    n_blocks = cap // MOE_BLOCK
    de = wg.shape[2]
    wspec = lambda a, b: pl.BlockSpec((None, a, b), lambda j, be, nu: (be[j], 0, 0))
    grid_spec = pltpu.PrefetchScalarGridSpec(
        num_scalar_prefetch=2,
        grid=(n_blocks,),
        in_specs=[pl.BlockSpec((MOE_BLOCK,) + row, lambda j, be, nu: (jnp.minimum(j, nu[0] - 1), 0, 0)),
                  wspec(d, de), wspec(d, de), wspec(de, d),
                  wspec(1, de), wspec(1, de), wspec(1, d)],
        out_specs=pl.BlockSpec((MOE_BLOCK,) + row, lambda j, be, nu: (j, 0, 0)),
        scratch_shapes=[pltpu.VMEM((d, de), BF16), pltpu.VMEM((d, de), BF16), pltpu.VMEM((de, d), BF16)],
    )
    return pl.pallas_call(
        _experts_kernel,
        grid_spec=grid_spec,
        out_shape=jax.ShapeDtypeStruct((cap,) + row, F32),
        compiler_params=pltpu.CompilerParams(dimension_semantics=("arbitrary",),
                                             vmem_limit_bytes=VMEM_LIMIT),
        name="experts",
    )(block_e, n_used, xg, wg, wu, wd, bg, bu, bd)


def _combine_kernel(dcur_ref, dnext_ref, yg_ref, gt_ref, x1_ref, g2_ref, nw_ref, o_ref, ybuf, sems):
    i = pl.program_id(0)
    last = pl.num_programs(0) - 1
    groups = TOK_TILE // ISSUE_UNROLL

    def copy(slot, k, t, d):
        return pltpu.make_async_copy(yg_ref.at[d], ybuf.at[slot, k, t], sems.at[slot])

    def issue(dref, slot):
        def body(g, c):
            for r in range(ISSUE_UNROLL):
                t = g * ISSUE_UNROLL + r
                for k in range(TOP_K):
                    copy(slot, k, t, dref[k, t]).start(priority=k % 2)
            return c

        lax.fori_loop(0, groups, body, 0)

    @pl.when(i == 0)
    def _():
        issue(dcur_ref, 0)

    @pl.when(i < last)
    def _():
        issue(dnext_ref, (i + 1) % 2)

    slot = i % 2

    def drain(g, c):
        for _ in range(ISSUE_UNROLL * TOP_K):
            copy(slot, 0, 0, 0).wait()
        return c

    lax.fori_loop(0, groups, drain, 0)
    d_model = o_ref.shape[-1]
    gt = gt_ref[...]
    f = gt[:, 0:1] * ybuf[slot, 0].reshape(TOK_TILE, d_model)
    for k in range(1, TOP_K):
        f = f + gt[:, k:k + 1] * ybuf[slot, k].reshape(TOK_TILE, d_model)
    ms = jnp.mean(f * f, axis=-1, keepdims=True)
    o_ref[...] = x1_ref[...] + g2_ref[...] * (f * lax.rsqrt(ms + EPS) * nw_ref[...])


def _combine(dest3, yg, gates_tok, x1, mod3, npf2, t_per_batch):
    n, d = x1.shape
    ntile = n // TOK_TILE
    per_b = t_per_batch // TOK_TILE
    return pl.pallas_call(
        _combine_kernel,
        grid=(ntile,),
        in_specs=[pl.BlockSpec((None, TOP_K, TOK_TILE), lambda i: (i, 0, 0), memory_space=pltpu.SMEM),
                  pl.BlockSpec((None, TOP_K, TOK_TILE), lambda i: (jnp.minimum(i + 1, ntile - 1), 0, 0),
                               memory_space=pltpu.SMEM),
                  pl.BlockSpec(memory_space=pl.ANY),
                  pl.BlockSpec((TOK_TILE, LANE), lambda i: (i, 0)),
                  pl.BlockSpec((TOK_TILE, d), lambda i: (i, 0)),
                  pl.BlockSpec((None, 1, d), lambda i: (i // per_b, 0, 5)),
                  pl.BlockSpec((1, d), lambda i: (0, 0))],
        out_specs=pl.BlockSpec((TOK_TILE, d), lambda i: (i, 0)),
        out_shape=jax.ShapeDtypeStruct((n, d), F32),
        scratch_shapes=[pltpu.VMEM((2, TOP_K, TOK_TILE) + yg.shape[1:], F32),
                        pltpu.SemaphoreType.DMA((2,))],
        compiler_params=pltpu.CompilerParams(dimension_semantics=("arbitrary",),
                                             vmem_limit_bytes=VMEM_LIMIT),
        name="combine",
    )(dest3, dest3, yg, gates_tok, x1, mod3, npf2)


def _pad_cols(w, cols):
    return jnp.pad(w, ((0, 0), (0, cols - w.shape[1])))


def kernel(x, c, ctx, c_ctx, w_ada, b_ada, norm_pre_mix, norm_post_mix, norm_pre_ffn, norm_post_ffn,
           w_in, gdn_conv_w, gdn_A_log, gdn_dt_bias, gdn_norm_w, ssd_conv_w, ssd_conv_b, ssd_A_log,
           ssd_dt_bias, ssd_D, ssd_norm_w, w_out, w_router, b_router, w_gate, b_gate, w_up, b_up,
           w_down, b_down):
    bn, t, d = x.shape
    n = bn * t
    l = 0
    mod_rows = -(-(bn + 1) // SUBLANE) * SUBLANE
    cc = jnp.zeros((mod_rows, d), F32).at[:bn].set(c).at[bn].set(c_ctx)
    mod3 = _ada(cc, w_ada[l], b_ada[l]).reshape(mod_rows, 1, 6 * d)

    w_in_b = w_in[l].astype(BF16)
    w_gdn = _pad_cols(w_in_b[:, :GDN_COLS], GDN_COLS_PAD)
    w_ssd = _pad_cols(w_in_b[:, GDN_COLS:], SSD_COLS_PAD)
    npm1 = norm_pre_mix[l].reshape(1, d)
    ug = _inproj(x, mod3, npm1, w_gdn, mod_row=None, col_major=False)
    us = _inproj(x, mod3, npm1, w_ssd, mod_row=None, col_major=True)
    ugc = _inproj(ctx, mod3, npm1, w_gdn, mod_row=bn, col_major=False)
    usc = _inproj(ctx, mod3, npm1, w_ssd, mod_row=bn, col_major=False)

    gdn_o = _gdn(ug, ugc, gdn_conv_w[l], _gdn_gate_params(gdn_A_log[l], gdn_dt_bias[l]),
                 jnp.tile(gdn_norm_w[l], 2).reshape(1, LANE))
    ssd_o = _ssd(us, usc, ssd_conv_w[l], ssd_conv_b[l].reshape(1, SSD_CONV_CH),
                 _ssd_gate_params(ssd_A_log[l], ssd_dt_bias[l]),
                 jnp.repeat(ssd_D[l], SSD_HEADDIM).reshape(SSD_GROUPS, 1, XW),
                 ssd_norm_w[l].reshape(SSD_GROUPS, 1, XW))

    x1, h2, e_t, gates_tok = _mix(
        gdn_o, ssd_o, x, mod3, norm_post_mix[l].reshape(1, d), norm_pre_ffn[l].reshape(1, d),
        w_out[l].astype(BF16), _pad_cols(w_router[l], LANE), _pad_cols(b_router[l].reshape(1, N_EXPERTS), LANE))

    n_blocks = -(-(n * TOP_K) // MOE_BLOCK) + N_EXPERTS
    cap = n_blocks * MOE_BLOCK
    ntile = n // TOK_TILE
    ids2 = e_t.reshape(SUBLANE * ntile, LANE)[:TOP_K * ntile]
    dest, meta = _rank(ids2, n_blocks)
    dest3 = dest.reshape(TOP_K, ntile, LANE).transpose(1, 0, 2)
    block_e = meta[:SUBLANE].reshape(-1)[:n_blocks]
    n_used = meta[SUBLANE, :1]

    xg = _dispatch(meta, dest3, h2, cap)
    yg = _experts(block_e, n_used, xg,
                  w_gate[l], w_up[l], w_down[l],
                  b_gate[l].reshape(N_EXPERTS, 1, D_EXPERT), b_up[l].reshape(N_EXPERTS, 1, D_EXPERT),
                  b_down[l].reshape(N_EXPERTS, 1, d))
    out = _combine(dest3, yg, gates_tok.reshape(n, LANE), x1.reshape(n, d), mod3,
                   norm_post_ffn[l].reshape(1, d), t)
    return out.reshape(bn, t, d)
```

```python
import functools

import jax
import jax.numpy as jnp
from jax import lax
from jax.experimental import pallas as pl
from jax.experimental.pallas import tpu as pltpu

F32 = jnp.float32
BF16 = jnp.bfloat16
I32 = jnp.int32

D_MODEL = 1024
GRID_W = 64
GDN_HEADS = 8
GDN_DK = 64
GDN_DV = 64
SSD_HEADS = 8
SSD_HEADDIM = 64
SSD_GROUPS = 2
SSD_STATE = 128
CONV_W = 5
CHUNK = 64
N_EXPERTS = 32
TOP_K = 4
D_EXPERT = 1024
SWIGLU_LIMIT = 7.0
SWIGLU_ALPHA = 1.702
MOE_BLOCK = 256
EPS = 1e-6

GDN_QK = GDN_HEADS * GDN_DK
GDN_V = GDN_HEADS * GDN_DV
GDN_CONV_CH = 2 * GDN_QK + GDN_V
GDN_COLS = GDN_CONV_CH + GDN_V + 4 * GDN_HEADS
SSD_INNER = SSD_HEADS * SSD_HEADDIM
SSD_BC = SSD_GROUPS * SSD_STATE
SSD_CONV_CH = SSD_INNER + 2 * SSD_BC
SSD_COLS = SSD_CONV_CH + SSD_INNER + 2 * SSD_HEADS

LANE = 128
SUBLANE = 8
GDN_COLS_PAD = 17 * LANE
SSD_COLS_PAD = 13 * LANE
VMEM_LIMIT = 56 * 1024 * 1024
NEG = -1e30
ROW_BLOCK = 256
INTRA_GROUP = 8
SSD_GROUP = 4


def _dot(a, b):
    return jnp.dot(a, b, preferred_element_type=F32)


def _dot_nt(a, b):
    return lax.dot_general(a, b, (((1,), (1,)), ((), ())), preferred_element_type=F32)


def _dot_f32(a, b):
    return jnp.dot(a, b, preferred_element_type=F32, precision=lax.Precision.HIGHEST)


def _silu(x):
    return x * (1.0 / (1.0 + jnp.exp(-x)))


def _sigmoid(x):
    return 1.0 / (1.0 + jnp.exp(-x))


def _softplus(x):
    return jnp.maximum(x, 0.0) + jnp.log(1.0 + jnp.exp(-jnp.abs(x)))


def _iota(shape, axis):
    return lax.broadcasted_iota(I32, shape, axis)


def _split_dot(x, ones_b16):
    hi = x.astype(BF16)
    lo = (x - hi.astype(F32)).astype(BF16)
    return _dot(hi, ones_b16) + _dot(lo, ones_b16)


def _ada_kernel(c_ref, w_ref, b_ref, o_ref):
    o_ref[...] = _dot_f32(_silu(c_ref[...]), w_ref[...]) + b_ref[...]


def _ada(cc, w_ada, b_ada):
    rows, d = cc.shape
    n = w_ada.shape[1]
    tn = 1024
    return pl.pallas_call(
        _ada_kernel,
        grid=(n // tn,),
        in_specs=[pl.BlockSpec((rows, d), lambda j: (0, 0)),
                  pl.BlockSpec((d, tn), lambda j: (0, j)),
                  pl.BlockSpec((1, tn), lambda j: (0, j))],
        out_specs=pl.BlockSpec((rows, tn), lambda j: (0, j)),
        out_shape=jax.ShapeDtypeStruct((rows, n), F32),
        compiler_params=pltpu.CompilerParams(dimension_semantics=("arbitrary",),
                                             vmem_limit_bytes=VMEM_LIMIT),
        name="ada",
    )(cc, w_ada, b_ada.reshape(1, n))


def _modulated_norm(x, nw, sh, sc):
    ms = jnp.mean(x * x, axis=-1, keepdims=True)
    return (x * lax.rsqrt(ms + EPS) * nw) * (1.0 + sc) + sh


def _inproj_kernel(x_ref, sh_ref, sc_ref, nw_ref, w_ref, o_ref, *, pieces):
    if pieces:
        xt = x_ref[...]
        x = jnp.concatenate([xt[:, i * D_MODEL:(i + 1) * D_MODEL] for i in range(pieces)], axis=0)
    else:
        x = x_ref[...]
    h = _modulated_norm(x, nw_ref[...], sh_ref[...], sc_ref[...])
    o_ref[...] = _dot(h.astype(BF16), w_ref[...])


def _inproj(x, mod3, nw, w_b16, *, mod_row, col_major):
    bn, t, d = x.shape
    cols = w_b16.shape[1]
    if col_major:
        rows = t // GRID_W
        pieces = 8
        tm = pieces * rows
        xin = x.reshape(bn, rows, GRID_W * d)
        x_spec = pl.BlockSpec((None, rows, pieces * d), lambda b, i: (b, 0, i))
    else:
        pieces = 0
        tm = min(512, t)
        xin = x
        x_spec = pl.BlockSpec((None, tm, d), lambda b, i: (b, i, 0))
    if mod_row is None:
        row = lambda b: b
    else:
        row = lambda b: mod_row
    return pl.pallas_call(
        functools.partial(_inproj_kernel, pieces=pieces),
        grid=(bn, t // tm),
        in_specs=[x_spec,
                  pl.BlockSpec((None, 1, d), lambda b, i: (row(b), 0, 0)),
                  pl.BlockSpec((None, 1, d), lambda b, i: (row(b), 0, 1)),
                  pl.BlockSpec((1, d), lambda b, i: (0, 0)),
                  pl.BlockSpec((d, cols), lambda b, i: (0, 0))],
        out_specs=pl.BlockSpec((None, tm, cols), lambda b, i: (b, i, 0)),
        out_shape=jax.ShapeDtypeStruct((bn, t, cols), F32),
        compiler_params=pltpu.CompilerParams(dimension_semantics=("arbitrary", "arbitrary"),
                                             vmem_limit_bytes=VMEM_LIMIT),
        name="inproj_cm" if col_major else "inproj",
    )(xin, mod3, mod3, nw, w_b16)


def _conv_silu(src_ref, t, pad_ref, w_ref, bias, dst_ref, post=None):
    _conv_silu_jobs([(src_ref, pad_ref, w_ref, bias, dst_ref, post)], t)


def _conv_silu_jobs(jobs, t):
    width = jobs[0][0].shape[-1]
    zero = jnp.zeros((SUBLANE, width), F32)
    rb = ROW_BLOCK * LANE // width
    nb = t // rb
    for src_ref, pad_ref, _, _, _, _ in jobs:
        pad_ref[0:SUBLANE, :] = zero
        pad_ref[pl.ds(SUBLANE + t, SUBLANE), :] = zero

    def copy(i, c):
        r0 = pl.multiple_of(i * rb, rb)
        for src_ref, pad_ref, _, _, _, _ in jobs:
            pad_ref[pl.ds(SUBLANE + r0, rb), :] = src_ref[pl.ds(r0, rb), :]
        return c

    lax.fori_loop(0, nb, copy, 0)
    ws = [job[2][...] for job in jobs]

    def body(i, c):
        r0 = pl.multiple_of(i * rb, rb)
        for (_, pad_ref, _, bias, dst_ref, post), w in zip(jobs, ws):
            win = pad_ref[pl.ds(r0, rb + 2 * SUBLANE), :]
            acc = None
            for j in range(CONV_W):
                off = SUBLANE - CONV_W // 2 + j
                tap = pltpu.roll(win, rb + 2 * SUBLANE - off, 0)[0:rb, :] * w[j:j + 1, :]
                acc = tap if acc is None else acc + tap
            if bias is not None:
                acc = acc + bias
            y = _silu(acc)
            if post is not None:
                y = post(y)
            dst_ref[pl.ds(r0, rb), :] = y
        return c

    lax.fori_loop(0, nb, body, 0)


def _chunk_cumsums(x, rows_in_chunk):
    n = x.shape[0]
    pre = x
    suf = x
    s = 1
    while s < CHUNK:
        pre = pre + jnp.where(rows_in_chunk >= s, pltpu.roll(pre, s, 0), 0.0)
        suf = suf + jnp.where(rows_in_chunk < CHUNK - s, pltpu.roll(suf, n - s, 0), 0.0)
        s *= 2
    return pre, suf


def _stack_masked(x, groups):
    r, n = x.shape
    w = n // groups
    lane_group = _iota((r, n), 1) // w
    return jnp.concatenate([jnp.where(lane_group == g, x, 0.0) for g in range(groups)], axis=0)


def _gdn_kernel(q_ref, k_ref, v_ref, z_ref, g_ref, kc_ref, vc_ref, gc_ref,
                wq_ref, wk_ref, wv_ref, gpar_ref, nw_ref, o_ref,
                pad_ref, pad2_ref, pad3_ref, qn_ref, kn_ref, vv_ref, gb_ref, gall_ref, gallc_ref,
                u_ref, w_ref, at_ref, qe_ref,
                m_ref, c_ref, ss_ref, egl_ref, oacc_ref, s_ref, *, t_lat, t_ctx):
    hp = pl.program_id(1)
    ones_bd = (_iota((LANE, LANE), 0) // GDN_DK == _iota((LANE, LANE), 1) // GDN_DK).astype(BF16)
    lane64 = _iota((CHUNK, LANE), 1)
    first = lane64 < GDN_DK
    ii = _iota((CHUNK, LANE), 0)
    jj = lane64 % CHUNK
    eye = (ii == jj).astype(F32)
    bd_mask = (_iota((LANE, LANE), 0) // GDN_DK) == (_iota((LANE, LANE), 1) // GDN_DV)
    gpar = gpar_ref[...]
    a_row = jnp.exp(gpar[0:1, :])
    dtb_row = gpar[1:2, :]

    def l2norm(scale):
        def f(y):
            ss = _split_dot(y * y, ones_bd)
            return y * lax.rsqrt(ss + EPS) * scale
        return f

    def gate_all_heads(src_ref, dst_ref, t):
        lane = _iota((ROW_BLOCK, LANE), 1)
        ric = _iota((ROW_BLOCK, LANE), 0) % CHUNK

        def body(i, c):
            r0 = pl.multiple_of(i * ROW_BLOCK, ROW_BLOCK)
            x = src_ref[pl.ds(r0, ROW_BLOCK), :]
            act = jnp.where(lane < 2 * GDN_HEADS, _sigmoid(x), -a_row * _softplus(x + dtb_row))
            pre, suf = _chunk_cumsums(act, ric)
            dst_ref[pl.ds(r0, ROW_BLOCK), :] = jnp.where(lane < 2 * GDN_HEADS, act,
                                                         jnp.where(lane < 3 * GDN_HEADS, pre, suf))
            return c

        lax.fori_loop(0, t // ROW_BLOCK, body, 0)

    @pl.when(hp == 0)
    def _():
        gate_all_heads(gc_ref, gallc_ref, t_ctx)
        gate_all_heads(g_ref, gall_ref, t_lat)

    def gate_block(src_ref, t):
        shift = (LANE - 2 * hp) % LANE

        def body(i, c):
            r0 = pl.multiple_of(i * ROW_BLOCK, ROW_BLOCK)
            gb_ref[pl.ds(r0, ROW_BLOCK), :] = pltpu.roll(src_ref[pl.ds(r0, ROW_BLOCK), :], shift, 1)
            return c

        lax.fori_loop(0, t // ROW_BLOCK, body, 0)

    def expand(tile, c0):
        return jnp.where(first, tile[:, c0:c0 + 1], tile[:, c0 + 1:c0 + 2])

    first2 = jnp.concatenate([first, first], axis=1)
    dirs = (0, 1)

    def intra(ns, with_out):
        r0s = [pl.multiple_of(n * CHUNK, CHUNK) for n in ns]
        ks = [kn_ref[pl.ds(r0, CHUNK), :] for r0 in r0s]
        gs = [gb_ref[pl.ds(r0, CHUNK), :] for r0 in r0s]
        gts = [jnp.concatenate([g, g], axis=0).T for g in gs]
        k_bds = [_stack_masked(k, 2).astype(BF16) for k in ks]
        probs = [(c, d) for c in range(len(ns)) for d in dirs]
        beta, gx, rel, gl, kb = {}, {}, {}, {}, {}
        for c, d in probs:
            g = gs[c]
            c0 = 2 * GDN_HEADS + GDN_HEADS * d
            beta[c, d] = expand(g, GDN_HEADS * d)
            gx[c, d] = expand(g, c0)
            grow = jnp.where(first[0:1, :], gts[c][c0:c0 + 1, :], gts[c][c0 + 1:c0 + 2, :])
            rel[c, d] = gx[c, d] - grow
            gl[c, d] = gx[c, d][CHUNK - 1:CHUNK, :] if d == 0 else gx[c, d][0:1, :]
            kb[c, d] = ks[c] * beta[c, d]
        strict = {0: ii > jj, 1: ii < jj}
        incl = {0: ii >= jj, 1: ii <= jj}
        lhs = []
        for c in range(len(ns)):
            parts = [kb[c, 0], kb[c, 1]]
            if with_out:
                parts.append(qn_ref[pl.ds(r0s[c], CHUNK), :])
            lhs.append(jnp.concatenate(parts, axis=0).astype(BF16))
        kk = [_dot_nt(lhs[c], k_bds[c]) for c in range(len(ns))]
        nm, p = {}, {}
        for c, d in probs:
            a = kk[c][d * CHUNK:(d + 1) * CHUNK, :] * jnp.exp(jnp.where(strict[d], rel[c, d], NEG))
            nm[c, d] = -a
            p[c, d] = eye + nm[c, d]
        for key in probs:
            nm[key] = _dot(nm[key].astype(BF16), _stack_masked(nm[key], 2).astype(BF16))
        for _ in range(4):
            for key in probs:
                both = _dot(jnp.concatenate([nm[key], p[key]], axis=0).astype(BF16),
                            _stack_masked(nm[key], 2).astype(BF16))
                nm[key] = both[0:CHUNK, :]
                p[key] = p[key] + both[CHUNK:2 * CHUNK, :]
        for key in probs:
            p[key] = p[key] + _dot(p[key].astype(BF16), _stack_masked(nm[key], 2).astype(BF16))
        vs = [vv_ref[pl.ds(r0, CHUNK), :] for r0 in r0s]
        eg, uw = {}, {}
        for c, d in probs:
            eg[c, d] = jnp.exp(gx[c, d])
            rhs = jnp.concatenate([vs[c] * beta[c, d], kb[c, d] * eg[c, d]], axis=1)
            rhs_bd = jnp.concatenate([jnp.where(first2, rhs, 0.0), jnp.where(first2, 0.0, rhs)], axis=0)
            uw[c, d] = _dot(p[c, d].astype(BF16), rhs_bd.astype(BF16))
        mc = {}
        for c, d in probs:
            kdec = ks[c] * jnp.exp(gl[c, d] - gx[c, d])
            kdt = jnp.concatenate([kdec, jnp.zeros_like(kdec)], axis=0).T
            wu = jnp.concatenate([uw[c, d][:, LANE:], uw[c, d][:, :LANE]], axis=1).astype(BF16)
            mc[c, d] = _dot(kdt.astype(BF16), jnp.concatenate([wu, jnp.zeros_like(wu)], axis=0))
        for c, d in probs:
            r0, n = r0s[c], ns[c]
            m0 = pl.multiple_of(n * LANE, LANE)
            m_ref[d, pl.ds(m0, LANE), :] = jnp.where(bd_mask, mc[c, d][:, :LANE], 0.0).astype(BF16)
            c_ref[d, pl.ds(m0, LANE), :] = jnp.where(bd_mask, mc[c, d][:, LANE:], 0.0)
            egl_ref[d, n] = jnp.broadcast_to(jnp.exp(gl[c, d]), (SUBLANE, LANE))
            if with_out:
                u_ref[d, pl.ds(r0, CHUNK), :] = uw[c, d][:, :LANE]
                w_ref[d, pl.ds(r0, CHUNK), :] = uw[c, d][:, LANE:].astype(BF16)
                at = kk[c][2 * CHUNK:3 * CHUNK, :] * jnp.exp(jnp.where(incl[d], rel[c, d], NEG))
                at_ref[d, pl.ds(r0, CHUNK), :] = at.astype(BF16)
                qe_ref[d, pl.ds(r0, CHUNK), :] = (qn_ref[pl.ds(r0, CHUNK), :] * eg[c, d]).astype(BF16)

    def scan_step(ns, with_out):
        m0s = [pl.multiple_of(n * LANE, LANE) for n in ns]
        s = [s_ref[d] for d in dirs]
        sb = [s[d].astype(BF16) for d in dirs]
        ms = [_dot(m_ref[d, pl.ds(m0s[d], LANE), :], sb[d]) for d in dirs]
        for d in dirs:
            s_ref[d] = s[d] * egl_ref[d, ns[d]][0:1, :] - ms[d] + c_ref[d, pl.ds(m0s[d], LANE), :]
            if with_out:
                ss_ref[d, pl.ds(m0s[d], LANE), :] = sb[d]

    def outputs(ns):
        cs = range(len(ns))
        probs = [(c, d) for c in cs for d in dirs]
        r0 = [pl.multiple_of(n * CHUNK, CHUNK) for n in ns]
        m0 = [pl.multiple_of(n * LANE, LANE) for n in ns]
        ws = {(c, d): _dot(jnp.concatenate([w_ref[d, pl.ds(r0[c], CHUNK), :], qe_ref[d, pl.ds(r0[c], CHUNK), :]],
                                           axis=0), ss_ref[d, pl.ds(m0[c], LANE), :]) for c, d in probs}
        vnew = {(c, d): u_ref[d, pl.ds(r0[c], CHUNK), :] - ws[c, d][0:CHUNK, :] for c, d in probs}
        av = {(c, d): _dot(at_ref[d, pl.ds(r0[c], CHUNK), :], _stack_masked(vnew[c, d], 2).astype(BF16))
              for c, d in probs}
        for c in cs:
            oacc_ref[pl.ds(r0[c], CHUNK), :] = (ws[c, 0][CHUNK:2 * CHUNK, :] + av[c, 0]
                                                + ws[c, 1][CHUNK:2 * CHUNK, :] + av[c, 1])

    def phase(t, k_src, v_src, g_src, q_src, with_out):
        nc = t // CHUNK
        jobs = [(k_src, pad_ref, wk_ref, None, kn_ref, l2norm(1.0)),
                (v_src, pad2_ref, wv_ref, None, vv_ref, None)]
        if with_out:
            jobs.append((q_src, pad3_ref, wq_ref, None, qn_ref, l2norm(GDN_DK ** -0.5)))
        _conv_silu_jobs(jobs, t)
        gate_block(g_src, t)

        group = min(INTRA_GROUP, nc)

        def intra_body(i, c):
            intra([i * group + j for j in range(group)], with_out)
            return c

        lax.fori_loop(0, nc // group, intra_body, 0)

        def scan_body(i, c):
            scan_step([i, nc - 1 - i], with_out)
            return c

        lax.fori_loop(0, nc, scan_body, 0)
        if with_out:
            def out_group(i, c):
                outputs([i * group + j for j in range(group)])
                return c

            lax.fori_loop(0, nc // group, out_group, 0)

    s_ref[...] = jnp.zeros(s_ref.shape, F32)
    phase(t_ctx, kc_ref, vc_ref, gallc_ref, None, False)
    phase(t_lat, k_ref, v_ref, gall_ref, q_ref, True)

    nw = nw_ref[...]

    def out_body(i, c):
        r0 = pl.multiple_of(i * ROW_BLOCK, ROW_BLOCK)
        o = oacc_ref[pl.ds(r0, ROW_BLOCK), :]
        ms = _split_dot(o * o, ones_bd) * (1.0 / GDN_DV)
        y = o * lax.rsqrt(ms + EPS) * nw * _silu(z_ref[pl.ds(r0, ROW_BLOCK), :])
        o_ref[pl.ds(r0, ROW_BLOCK), :] = y.astype(o_ref.dtype)
        return c

    lax.fori_loop(0, t_lat // ROW_BLOCK, out_body, 0)


def _gdn(u_lat, u_ctx, conv_w, gpar, nw2):
    bn, t, _ = u_lat.shape
    tc = u_ctx.shape[1]
    npairs = GDN_HEADS // 2
    nc = t // CHUNK
    col = lambda off: (lambda b, h: (b, 0, off + h))
    lat = lambda off: pl.BlockSpec((None, t, LANE), col(off))
    ctx = lambda off: pl.BlockSpec((None, tc, LANE), col(off))
    gate_blk = GDN_CONV_CH // LANE + GDN_V // LANE
    fixed = lambda b, h: (b, 0, gate_blk)
    cw = lambda off: pl.BlockSpec((CONV_W, LANE), lambda b, h: (0, off + h))
    return pl.pallas_call(
        functools.partial(_gdn_kernel, t_lat=t, t_ctx=tc),
        grid=(bn, npairs),
        in_specs=[lat(0), lat(npairs), lat(2 * npairs), lat(3 * npairs),
                  pl.BlockSpec((None, t, LANE), fixed),
                  ctx(npairs), ctx(2 * npairs), pl.BlockSpec((None, tc, LANE), fixed),
                  cw(0), cw(npairs), cw(2 * npairs),
                  pl.BlockSpec((SUBLANE, LANE), lambda b, h: (0, 0)),
                  pl.BlockSpec((1, LANE), lambda b, h: (0, 0))],
        out_specs=pl.BlockSpec((None, t, LANE), lambda b, h: (b, 0, h)),
        out_shape=jax.ShapeDtypeStruct((bn, t, GDN_V), BF16),
        scratch_shapes=[pltpu.VMEM((t + 2 * SUBLANE, LANE), F32),
                        pltpu.VMEM((t + 2 * SUBLANE, LANE), F32),
                        pltpu.VMEM((t + 2 * SUBLANE, LANE), F32),
                        pltpu.VMEM((t, LANE), F32),
                        pltpu.VMEM((t, LANE), F32),
                        pltpu.VMEM((t, LANE), F32),
                        pltpu.VMEM((t, LANE), F32),
                        pltpu.VMEM((t, LANE), F32),
                        pltpu.VMEM((tc, LANE), F32),
                        pltpu.VMEM((2, t, LANE), F32),
                        pltpu.VMEM((2, t, LANE), BF16),
                        pltpu.VMEM((2, t, LANE), BF16),
                        pltpu.VMEM((2, t, LANE), BF16),
                        pltpu.VMEM((2, nc * LANE, LANE), BF16),
                        pltpu.VMEM((2, nc * LANE, LANE), F32),
                        pltpu.VMEM((2, nc * LANE, LANE), BF16),
                        pltpu.VMEM((2, nc, SUBLANE, LANE), F32),
                        pltpu.VMEM((t, LANE), F32),
                        pltpu.VMEM((2, LANE, LANE), F32)],
        compiler_params=pltpu.CompilerParams(dimension_semantics=("arbitrary", "arbitrary"),
                                             vmem_limit_bytes=VMEM_LIMIT),
        name="gdn",
    )(u_lat, u_lat, u_lat, u_lat, u_lat, u_ctx, u_ctx, u_ctx, conv_w, conv_w, conv_w, gpar, nw2)


HG = SSD_HEADS // SSD_GROUPS
XW = HG * SSD_HEADDIM


def _ssd_kernel(x_ref, b_ref, c_ref, z_ref, g_ref, xc_ref, bc_ref, gc_ref,
                wx_ref, wb_ref, wc_ref, bx_ref, bb_ref, bcb_ref, gpar_ref, dsk_ref, nw_ref, o_ref,
                padx_ref, padb_ref, padc_ref, xs_ref, bs_ref, cs_ref, gb_ref, yacc_ref, s_ref, *, t_lat, t_ctx):
    grp = pl.program_id(1)
    lane4 = _iota((CHUNK, XW), 1)
    lg = lane4 // SSD_HEADDIM
    ii = _iota((CHUNK, XW), 0)
    jj = lane4 % CHUNK
    gpar = gpar_ref[...]
    a_row = -jnp.exp(gpar[0:1, :])
    dtb_row = gpar[1:2, :]

    def gate_block(src_ref, t):
        lane = _iota((ROW_BLOCK, LANE), 1)
        ric = _iota((ROW_BLOCK, LANE), 0) % CHUNK
        shift = (LANE - HG * grp) % LANE

        def body(i, c):
            r0 = pl.multiple_of(i * ROW_BLOCK, ROW_BLOCK)
            x = src_ref[pl.ds(r0, ROW_BLOCK), :]
            dt = _softplus(x + dtb_row)
            a = pltpu.roll(dt * a_row, 2 * SSD_HEADS, 1)
            pre, suf = _chunk_cumsums(a, ric)
            blk = jnp.where(lane < 2 * SSD_HEADS, dt, jnp.where(lane < 3 * SSD_HEADS, pre, suf))
            gb_ref[pl.ds(r0, ROW_BLOCK), :] = pltpu.roll(blk, shift, 1)
            return c

        lax.fori_loop(0, t // ROW_BLOCK, body, 0)

    def expand(tile, c0):
        out = tile[:, c0 + HG - 1:c0 + HG]
        for h in range(HG - 2, -1, -1):
            out = jnp.where(lg == h, tile[:, c0 + h:c0 + h + 1], out)
        return out

    def rowsel(gt2, c0):
        out = gt2[c0 + HG - 1:c0 + HG, :]
        for h in range(HG - 2, -1, -1):
            out = jnp.where(lg[0:1, :] == h, gt2[c0 + h:c0 + h + 1, :], out)
        return out

    def step(chunks, with_out):
        probs = [(d, j) for d in (0, 1) for j in range(len(chunks[d]))]
        r0, ax, al, x, bm, cm, xw, upd, yin = {}, {}, {}, {}, {}, {}, {}, {}, {}
        for key in probs:
            d, j = key
            r0[key] = pl.multiple_of(chunks[d][j] * CHUNK, CHUNK)
            g = gb_ref[pl.ds(r0[key], CHUNK), :]
            dtx = expand(g, SSD_HEADS * d)
            ax[key] = expand(g, 2 * SSD_HEADS + SSD_HEADS * d)
            al[key] = ax[key][CHUNK - 1:CHUNK, :] if d == 0 else ax[key][0:1, :]
            x[key] = xs_ref[pl.ds(r0[key], CHUNK), :]
            bm[key] = bs_ref[pl.ds(r0[key], CHUNK), :]
            xw[key] = (x[key] * (jnp.exp(al[key] - ax[key]) * dtx)).astype(BF16)
            if with_out:
                gt = jnp.concatenate([g, g], axis=0).T
                gt2 = jnp.concatenate([gt, gt], axis=1)
                arow = rowsel(gt2, 2 * SSD_HEADS + SSD_HEADS * d)
                dtrow = rowsel(gt2, SSD_HEADS * d)
                incl = (ii >= jj) if d == 0 else (ii <= jj)
                yin[key] = jnp.exp(jnp.where(incl, ax[key] - arow, NEG)) * dtrow
                cm[key] = cs_ref[pl.ds(r0[key], CHUNK), :].astype(BF16)
        if with_out:
            cb = {key: _dot_nt(cm[key], jnp.concatenate([bm[key]] * HG, axis=0).astype(BF16)) for key in probs}
        for key in probs:
            bt = jnp.concatenate([bm[key], jnp.zeros_like(bm[key])], axis=0).T.astype(BF16)
            upd[key] = _dot(bt, jnp.concatenate([xw[key], jnp.zeros_like(xw[key])], axis=0))
        if with_out:
            for key in probs:
                yin[key] = _dot((cb[key] * yin[key]).astype(BF16), _stack_masked(x[key], HG).astype(BF16))
        for d in (0, 1):
            s = s_ref[d]
            for j in range(len(chunks[d])):
                key = (d, j)
                if with_out:
                    y = yin[key] + _dot(cm[key], s.astype(BF16)) * jnp.exp(ax[key])
                    yacc_ref[pl.ds(r0[key], CHUNK), :] += y
                s = s * jnp.exp(al[key]) + upd[key]
            s_ref[d] = s

    def phase(t, x_src, b_src, c_src, g_src, with_out):
        nc = t // CHUNK
        _conv_silu(x_src, t, padx_ref, wx_ref, bx_ref[...], xs_ref)
        jobs = [(b_src, padb_ref, wb_ref, bb_ref[...], bs_ref, None)]
        if with_out:
            jobs.append((c_src, padc_ref, wc_ref, bcb_ref[...], cs_ref, None))
        _conv_silu_jobs(jobs, t)
        gate_block(g_src, t)

        def body(i, c):
            fwd = [i * SSD_GROUP + j for j in range(SSD_GROUP)]
            bwd = [nc - 1 - i * SSD_GROUP - j for j in range(SSD_GROUP)]
            step([fwd, bwd], with_out)
            return c

        lax.fori_loop(0, nc // SSD_GROUP, body, 0)

    s_ref[...] = jnp.zeros(s_ref.shape, F32)
    yacc_ref[...] = jnp.zeros(yacc_ref.shape, F32)
    phase(t_ctx, xc_ref, bc_ref, None, gc_ref, False)
    phase(t_lat, x_ref, b_ref, c_ref, g_ref, True)

    dsk = dsk_ref[...]
    nw = nw_ref[...]

    rb = ROW_BLOCK * LANE // XW

    def out_body(i, c):
        r0 = pl.multiple_of(i * rb, rb)
        y = yacc_ref[pl.ds(r0, rb), :] + dsk * xs_ref[pl.ds(r0, rb), :]
        y = y * _silu(z_ref[pl.ds(r0, rb), :])
        ms = jnp.mean(y * y, axis=-1, keepdims=True)
        o_ref[pl.ds(r0, rb), :] = (y * lax.rsqrt(ms + EPS) * nw).astype(o_ref.dtype)
        return c

    lax.fori_loop(0, t_lat // rb, out_body, 0)


def _ssd(u_lat, u_ctx, conv_w, conv_b, gpar, dskip, nw):
    bn, t, _ = u_lat.shape
    tc = u_ctx.shape[1]
    x_blk = lambda tt: pl.BlockSpec((None, tt, XW), lambda b, g: (b, 0, g))
    n_blk = lambda tt, off: pl.BlockSpec((None, tt, LANE), lambda b, g: (b, 0, off + g))
    b_off = SSD_INNER // LANE
    c_off = b_off + SSD_BC // LANE
    z_off = SSD_CONV_CH // XW
    dt_blk = (SSD_CONV_CH + SSD_INNER) // LANE
    fixed = lambda b, g: (b, 0, dt_blk)
    return pl.pallas_call(
        functools.partial(_ssd_kernel, t_lat=t, t_ctx=tc),
        grid=(bn, SSD_GROUPS),
        in_specs=[x_blk(t), n_blk(t, b_off), n_blk(t, c_off),
                  pl.BlockSpec((None, t, XW), lambda b, g: (b, 0, z_off + g)),
                  pl.BlockSpec((None, t, LANE), fixed),
                  x_blk(tc), n_blk(tc, b_off), pl.BlockSpec((None, tc, LANE), fixed),
                  pl.BlockSpec((CONV_W, XW), lambda b, g: (0, g)),
                  pl.BlockSpec((CONV_W, LANE), lambda b, g: (0, b_off + g)),
                  pl.BlockSpec((CONV_W, LANE), lambda b, g: (0, c_off + g)),
                  pl.BlockSpec((1, XW), lambda b, g: (0, g)),
                  pl.BlockSpec((1, LANE), lambda b, g: (0, b_off + g)),
                  pl.BlockSpec((1, LANE), lambda b, g: (0, c_off + g)),
                  pl.BlockSpec((SUBLANE, LANE), lambda b, g: (0, 0)),
                  pl.BlockSpec((None, 1, XW), lambda b, g: (g, 0, 0)),
                  pl.BlockSpec((None, 1, XW), lambda b, g: (g, 0, 0))],
        out_specs=pl.BlockSpec((None, t, XW), lambda b, g: (b, 0, g)),
        out_shape=jax.ShapeDtypeStruct((bn, t, SSD_INNER), BF16),
        scratch_shapes=[pltpu.VMEM((t + 2 * SUBLANE, XW), F32),
                        pltpu.VMEM((t + 2 * SUBLANE, LANE), F32),
                        pltpu.VMEM((t + 2 * SUBLANE, LANE), F32),
                        pltpu.VMEM((t, XW), F32),
                        pltpu.VMEM((t, LANE), F32),
                        pltpu.VMEM((t, LANE), F32),
                        pltpu.VMEM((t, LANE), F32),
                        pltpu.VMEM((t, XW), F32),
                        pltpu.VMEM((2, SSD_STATE, XW), F32)],
        compiler_params=pltpu.CompilerParams(dimension_semantics=("arbitrary", "arbitrary"),
                                             vmem_limit_bytes=VMEM_LIMIT),
        name="ssd",
    )(u_lat, u_lat, u_lat, u_lat, u_lat, u_ctx, u_ctx, u_ctx,
      conv_w, conv_w, conv_w, conv_b, conv_b, conv_b, gpar, dskip, nw)


def _gdn_gate_params(a_log, dt_bias):
    z = jnp.zeros((SUBLANE, LANE), F32)
    z = z.at[0, 2 * GDN_HEADS:4 * GDN_HEADS].set(a_log.reshape(-1))
    return z.at[1, 2 * GDN_HEADS:4 * GDN_HEADS].set(dt_bias.reshape(-1))


def _ssd_gate_params(a_log, dt_bias):
    z = jnp.zeros((SUBLANE, LANE), F32)
    z = z.at[0, 0:2 * SSD_HEADS].set(a_log.reshape(-1))
    return z.at[1, 0:2 * SSD_HEADS].set(dt_bias.reshape(-1))


MIX_TM = 256
TOK_TILE = 128


def _mix_kernel(go_ref, so_ref, x_ref, g1_ref, sh2_ref, sc2_ref, npm_ref, npf_ref, wo_ref, wr_ref, br_ref,
                x1_ref, h2_ref, et_ref, gt_ref):
    so = so_ref[...]
    rows = MIX_TM // GRID_W
    ssd = jnp.concatenate([so[:, r * SSD_INNER:(r + 1) * SSD_INNER] for r in range(rows)], axis=0)
    m = _dot(go_ref[...], wo_ref[0:GDN_V, :]) + _dot(ssd, wo_ref[GDN_V:GDN_V + SSD_INNER, :])
    ms = jnp.mean(m * m, axis=-1, keepdims=True)
    x1 = x_ref[...] + g1_ref[...] * (m * lax.rsqrt(ms + EPS) * npm_ref[...])
    x1_ref[...] = x1
    h2 = _modulated_norm(x1, npf_ref[...], sh2_ref[...], sc2_ref[...])
    h2_ref[...] = h2.reshape(h2_ref.shape)
    wr = wr_ref[...]
    h_hi = h2.astype(BF16)
    h_lo = (h2 - h_hi.astype(F32)).astype(BF16)
    w_hi = wr.astype(BF16)
    w_lo = (wr - w_hi.astype(F32)).astype(BF16)
    logits = _dot(h_hi, w_hi) + (_dot(h_hi, w_lo) + _dot(h_lo, w_hi)) + br_ref[...]
    lane = _iota(logits.shape, 1)
    lane_f = lane.astype(F32)
    l = jnp.where(lane < N_EXPERTS, logits, NEG)
    vals = []
    for _ in range(TOP_K):
        mk = jnp.max(l, axis=1, keepdims=True)
        idx = jnp.min(jnp.where(l == mk, lane_f, float(LANE)), axis=1, keepdims=True)
        l = jnp.where(lane_f == idx, NEG, l)
        vals.append(mk)
    ex = [jnp.exp(v - vals[0]) for v in vals]
    den = ex[0] + ex[1] + ex[2] + ex[3]
    gates = jnp.zeros(logits.shape, F32)
    for k in range(TOP_K):
        gates = jnp.where(lane == k, ex[k] / den, gates)
    gt_ref[...] = gates
    lt = logits.T[0:N_EXPERTS, :]
    row_f = _iota(lt.shape, 0).astype(F32)
    ids = []
    for _ in range(TOP_K):
        mk = jnp.max(lt, axis=0, keepdims=True)
        idx = jnp.min(jnp.where(lt == mk, row_f, float(N_EXPERTS)), axis=0, keepdims=True)
        lt = jnp.where(row_f == idx, NEG, lt)
        ids.append(idx.astype(I32))
    pad = jnp.full((SUBLANE - TOP_K, lt.shape[1]), -1, I32)
    et_ref[...] = jnp.concatenate(ids + [pad], axis=0)


def _mix(gdn_o, ssd_o, x, mod3, npm, npf, w_out_b16, w_router_pad, b_router_pad):
    bn, t, d = x.shape
    n = bn * t
    rows = MIX_TM // GRID_W
    nt = t // MIX_TM
    ssd_v = ssd_o.reshape(bn, GRID_W, (t // GRID_W) * SSD_INNER)
    modrow = lambda blk: pl.BlockSpec((None, 1, d), lambda b, i: (b, 0, blk))
    return pl.pallas_call(
        _mix_kernel,
        grid=(bn, nt),
        in_specs=[pl.BlockSpec((None, MIX_TM, GDN_V), lambda b, i: (b, i, 0)),
                  pl.BlockSpec((None, GRID_W, rows * SSD_INNER), lambda b, i: (b, 0, i)),
                  pl.BlockSpec((None, MIX_TM, d), lambda b, i: (b, i, 0)),
                  modrow(2), modrow(3), modrow(4),
                  pl.BlockSpec((1, d), lambda b, i: (0, 0)),
                  pl.BlockSpec((1, d), lambda b, i: (0, 0)),
                  pl.BlockSpec((GDN_V + SSD_INNER, d), lambda b, i: (0, 0)),
                  pl.BlockSpec((d, LANE), lambda b, i: (0, 0)),
                  pl.BlockSpec((1, LANE), lambda b, i: (0, 0))],
        out_specs=[pl.BlockSpec((None, MIX_TM, d), lambda b, i: (b, i, 0)),
                   pl.BlockSpec((MIX_TM, SUBLANE, d // SUBLANE), lambda b, i: (b * nt + i, 0, 0)),
                   pl.BlockSpec((SUBLANE, MIX_TM), lambda b, i: (0, b * nt + i)),
                   pl.BlockSpec((None, MIX_TM, LANE), lambda b, i: (b, i, 0))],
        out_shape=[jax.ShapeDtypeStruct((bn, t, d), F32),
                   jax.ShapeDtypeStruct((n, SUBLANE, d // SUBLANE), F32),
                   jax.ShapeDtypeStruct((SUBLANE, n), I32),
                   jax.ShapeDtypeStruct((bn, t, LANE), F32)],
        compiler_params=pltpu.CompilerParams(dimension_semantics=("arbitrary", "arbitrary"),
                                             vmem_limit_bytes=VMEM_LIMIT),
        name="mix",
    )(gdn_o, ssd_v, x, mod3, mod3, mod3, npm, npf, w_out_b16, w_router_pad, b_router_pad)


def _rank_kernel(ids_ref, dest_ref, meta_ref, tril_ref, cnt_ref, rank_ref, *, n_blocks):
    nrow = ids_ref.shape[0]
    ids = ids_ref[...]
    tril_ref[...] = (_iota((nrow, nrow), 0) > _iota((nrow, nrow), 1)).astype(BF16)
    upper = (_iota((LANE, LANE), 0) < _iota((LANE, LANE), 1)).astype(BF16)
    ones = jnp.ones((LANE, LANE), BF16)
    rank_ref[...] = jnp.zeros(rank_ref.shape, F32)

    def count(e, c):
        hit = ids == e
        mask = hit.astype(F32).astype(BF16)
        before = _dot(mask, upper)
        rowsum = _dot(mask, ones)
        rows_before = _dot(tril_ref[...], rowsum.astype(BF16))
        rank_ref[...] += jnp.where(hit, before + rows_before, 0.0)
        cnt_ref[pl.ds(e, 1), :] = rows_before[nrow - 1:nrow, :] + rowsum[nrow - 1:nrow, :]
        return c

    lax.fori_loop(0, N_EXPERTS, count, 0)
    cnt = cnt_ref[...]
    padded = jnp.floor((cnt + (MOE_BLOCK - 1)) * (1.0 / MOE_BLOCK)) * MOE_BLOCK
    erow = _iota(cnt.shape, 0)
    pad_end = padded
    s = 1
    while s < N_EXPERTS:
        pad_end = pad_end + jnp.where(erow >= s, pltpu.roll(pad_end, s, 0), 0.0)
        s *= 2
    cnt_ref[...] = pad_end - padded
    jblk = (_iota((SUBLANE, LANE), 0) * LANE + _iota((SUBLANE, LANE), 1)).astype(F32) * MOE_BLOCK
    block_e = jnp.zeros((SUBLANE, LANE), F32)
    for e in range(N_EXPERTS):
        block_e = block_e + (pad_end[e:e + 1, :] <= jblk).astype(F32)
    block_e = jnp.minimum(block_e, N_EXPERTS - 1.0)
    n_used = pad_end[N_EXPERTS - 1:N_EXPERTS, :] * (1.0 / MOE_BLOCK)
    on_diag = erow == _iota(cnt.shape, 1)
    pad_end_l = jnp.sum(jnp.where(on_diag, pad_end, 0.0), axis=0, keepdims=True)
    cnt_l = jnp.sum(jnp.where(on_diag, cnt, 0.0), axis=0, keepdims=True)
    r8 = _iota((SUBLANE, LANE), 0)
    tail = jnp.where(r8 == 0, n_used, jnp.where(r8 == 1, pad_end_l, jnp.where(r8 == 2, cnt_l, 0.0)))
    meta_ref[...] = jnp.concatenate([block_e, tail], axis=0).astype(I32)

    def place(e, c):
        rank_ref[...] += jnp.where(ids == e, cnt_ref[pl.ds(e, 1), :], 0.0)
        return c

    lax.fori_loop(0, N_EXPERTS, place, 0)
    dest_ref[...] = rank_ref[...].astype(I32)


def _rank(ids2, n_blocks):
    nrow = ids2.shape[0]
    return pl.pallas_call(
        functools.partial(_rank_kernel, n_blocks=n_blocks),
        in_specs=[pl.BlockSpec((nrow, LANE), lambda: (0, 0))],
        out_specs=[pl.BlockSpec((nrow, LANE), lambda: (0, 0)),
                   pl.BlockSpec((2 * SUBLANE, LANE), lambda: (0, 0))],
        out_shape=[jax.ShapeDtypeStruct((nrow, LANE), I32),
                   jax.ShapeDtypeStruct((2 * SUBLANE, LANE), I32)],
        scratch_shapes=[pltpu.VMEM((nrow, nrow), BF16),
                        pltpu.VMEM((N_EXPERTS, LANE), F32),
                        pltpu.VMEM((nrow, LANE), F32)],
        compiler_params=pltpu.CompilerParams(vmem_limit_bytes=VMEM_LIMIT),
        name="rank",
    )(ids2)


EXPERT_SPLIT = 2


ISSUE_UNROLL = 8


META_N_USED = SUBLANE
META_PAD_END = SUBLANE + 1
META_COUNT = SUBLANE + 2


def _dispatch_kernel(meta_ref, dest_ref, h_ref, xg_ref, zero_ref, sem, zsem):
    n_blocks = xg_ref.shape[0] // MOE_BLOCK

    def zero_copy(start):
        return pltpu.make_async_copy(zero_ref, xg_ref.at[pl.ds(start, MOE_BLOCK)], zsem)

    def zero_fill(action):
        for e in range(N_EXPERTS):
            @pl.when(meta_ref[META_COUNT, e] > 0)
            def _(e=e):
                action(zero_copy(meta_ref[META_PAD_END, e] - MOE_BLOCK))
        for j in range(n_blocks - N_EXPERTS, n_blocks):
            @pl.when(j >= meta_ref[META_N_USED, 0])
            def _(j=j):
                action(zero_copy(j * MOE_BLOCK))

    @pl.when(pl.program_id(0) == 0)
    def _():
        zero_ref[...] = jnp.zeros(zero_ref.shape, zero_ref.dtype)
        zero_fill(lambda cp: cp.start())
        zero_fill(lambda cp: cp.wait())

    def copy(t, d):
        return pltpu.make_async_copy(h_ref.at[t], xg_ref.at[d], sem)

    def issue(i, c):
        for r in range(ISSUE_UNROLL):
            t = i * ISSUE_UNROLL + r
            for k in range(TOP_K):
                copy(t, dest_ref[k, t]).start(priority=k % 2)
        return c

    lax.fori_loop(0, TOK_TILE // ISSUE_UNROLL, issue, 0)

    def drain(i, c):
        for _ in range(ISSUE_UNROLL * TOP_K):
            copy(0, 0).wait()
        return c

    lax.fori_loop(0, TOK_TILE // ISSUE_UNROLL, drain, 0)


def _dispatch(meta, dest3, h3, cap):
    ntile = dest3.shape[0]
    row = h3.shape[1:]
    return pl.pallas_call(
        _dispatch_kernel,
        grid=(ntile,),
        in_specs=[pl.BlockSpec(meta.shape, lambda i: (0, 0), memory_space=pltpu.SMEM),
                  pl.BlockSpec((None, TOP_K, TOK_TILE), lambda i: (i, 0, 0), memory_space=pltpu.SMEM),
                  pl.BlockSpec((TOK_TILE,) + row, lambda i: (i, 0, 0))],
        out_specs=pl.BlockSpec(memory_space=pl.ANY),
        out_shape=jax.ShapeDtypeStruct((cap,) + row, h3.dtype),
        scratch_shapes=[pltpu.VMEM((MOE_BLOCK,) + row, h3.dtype),
                        pltpu.SemaphoreType.DMA(()),
                        pltpu.SemaphoreType.DMA(())],
        compiler_params=pltpu.CompilerParams(dimension_semantics=("arbitrary",)),
        name="dispatch",
    )(meta, dest3, h3)


def _experts_kernel(be_ref, nu_ref, x_ref, wg_ref, wu_ref, wd_ref, bg_ref, bu_ref, bd_ref, y_ref,
                    wgb_ref, wub_ref, wdb_ref):
    j = pl.program_id(0)
    new_expert = jnp.logical_or(j == 0, be_ref[j] != be_ref[jnp.maximum(j - 1, 0)])

    @pl.when(jnp.logical_and(new_expert, j < nu_ref[0]))
    def _():
        rows = wg_ref.shape[0] // 8
        for src, dst in ((wg_ref, wgb_ref), (wu_ref, wub_ref), (wd_ref, wdb_ref)):
            def cast(i, c, src=src, dst=dst):
                r0 = pl.multiple_of(i * rows, rows)
                dst[pl.ds(r0, rows), :] = src[pl.ds(r0, rows), :].astype(BF16)
                return c
            lax.fori_loop(0, 8, cast, 0)

    @pl.when(j < nu_ref[0])
    def _():
        x = x_ref[...].reshape(x_ref.shape[0], wgb_ref.shape[0]).astype(BF16)
        de = wgb_ref.shape[1]
        w = de // EXPERT_SPLIT
        ab = []
        for s in range(EXPERT_SPLIT):
            cols = slice(s * w, (s + 1) * w)
            ab.append((_dot(x, wgb_ref[:, cols]) + bg_ref[:, cols], _dot(x, wub_ref[:, cols]) + bu_ref[:, cols]))
        y = bd_ref[...]
        for s in range(EXPERT_SPLIT):
            a = jnp.minimum(ab[s][0], SWIGLU_LIMIT)
            b = jnp.clip(ab[s][1], -SWIGLU_LIMIT, SWIGLU_LIMIT)
            hid = a * _sigmoid(SWIGLU_ALPHA * a) * (b + 1.0)
            y = y + _dot(hid.astype(BF16), wdb_ref[s * w:(s + 1) * w, :])
        y_ref[...] = y.reshape(y_ref.shape)

    @pl.when(j >= nu_ref[0])
    def _():
        y_ref[...] = jnp.zeros(y_ref.shape, y_ref.dtype)


def _experts(block_e, n_used, xg, wg, wu, wd, bg, bu, bd):
    cap = xg.shape[0]
    d = wg.shape[1]
    row = xg.shape[1:]
    n_blocks = cap // MOE_BLOCK
    de = wg.shape[2]
    wspec = lambda a, b: pl.BlockSpec((None, a, b), lambda j, be, nu: (be[j], 0, 0))
    grid_spec = pltpu.PrefetchScalarGridSpec(
        num_scalar_prefetch=2,
        grid=(n_blocks,),
        in_specs=[pl.BlockSpec((MOE_BLOCK,) + row, lambda j, be, nu: (jnp.minimum(j, nu[0] - 1), 0, 0)),
                  wspec(d, de), wspec(d, de), wspec(de, d),
                  wspec(1, de), wspec(1, de), wspec(1, d)],
        out_specs=pl.BlockSpec((MOE_BLOCK,) + row, lambda j, be, nu: (j, 0, 0)),
        scratch_shapes=[pltpu.VMEM((d, de), BF16), pltpu.VMEM((d, de), BF16), pltpu.VMEM((de, d), BF16)],
    )
    return pl.pallas_call(
        _experts_kernel,
        grid_spec=grid_spec,
        out_shape=jax.ShapeDtypeStruct((cap,) + row, F32),
        compiler_params=pltpu.CompilerParams(dimension_semantics=("arbitrary",),
                                             vmem_limit_bytes=VMEM_LIMIT),
        name="experts",
    )(block_e, n_used, xg, wg, wu, wd, bg, bu, bd)


def _combine_kernel(dcur_ref, dnext_ref, yg_ref, gt_ref, x1_ref, g2_ref, nw_ref, o_ref, ybuf, sems):
    i = pl.program_id(0)
    last = pl.num_programs(0) - 1
    groups = TOK_TILE // ISSUE_UNROLL

    def copy(slot, k, t, d):
        return pltpu.make_async_copy(yg_ref.at[d], ybuf.at[slot, k, t], sems.at[slot])

    def issue(dref, slot):
        def body(g, c):
            for r in range(ISSUE_UNROLL):
                t = g * ISSUE_UNROLL + r
                for k in range(TOP_K):
                    copy(slot, k, t, dref[k, t]).start(priority=k % 2)
            return c

        lax.fori_loop(0, groups, body, 0)

    @pl.when(i == 0)
    def _():
        issue(dcur_ref, 0)

    @pl.when(i < last)
    def _():
        issue(dnext_ref, (i + 1) % 2)

    slot = i % 2

    def drain(g, c):
        for _ in range(ISSUE_UNROLL * TOP_K):
            copy(slot, 0, 0, 0).wait()
        return c

    lax.fori_loop(0, groups, drain, 0)
    d_model = o_ref.shape[-1]
    gt = gt_ref[...]
    f = gt[:, 0:1] * ybuf[slot, 0].reshape(TOK_TILE, d_model)
    for k in range(1, TOP_K):
        f = f + gt[:, k:k + 1] * ybuf[slot, k].reshape(TOK_TILE, d_model)
    ms = jnp.mean(f * f, axis=-1, keepdims=True)
    o_ref[...] = x1_ref[...] + g2_ref[...] * (f * lax.rsqrt(ms + EPS) * nw_ref[...])


def _combine(dest3, yg, gates_tok, x1, mod3, npf2, t_per_batch):
    n, d = x1.shape
    ntile = n // TOK_TILE
    per_b = t_per_batch // TOK_TILE
    return pl.pallas_call(
        _combine_kernel,
        grid=(ntile,),
        in_specs=[pl.BlockSpec((None, TOP_K, TOK_TILE), lambda i: (i, 0, 0), memory_space=pltpu.SMEM),
                  pl.BlockSpec((None, TOP_K, TOK_TILE), lambda i: (jnp.minimum(i + 1, ntile - 1), 0, 0),
                               memory_space=pltpu.SMEM),
                  pl.BlockSpec(memory_space=pl.ANY),
                  pl.BlockSpec((TOK_TILE, LANE), lambda i: (i, 0)),
                  pl.BlockSpec((TOK_TILE, d), lambda i: (i, 0)),
                  pl.BlockSpec((None, 1, d), lambda i: (i // per_b, 0, 5)),
                  pl.BlockSpec((1, d), lambda i: (0, 0))],
        out_specs=pl.BlockSpec((TOK_TILE, d), lambda i: (i, 0)),
        out_shape=jax.ShapeDtypeStruct((n, d), F32),
        scratch_shapes=[pltpu.VMEM((2, TOP_K, TOK_TILE) + yg.shape[1:], F32),
                        pltpu.SemaphoreType.DMA((2,))],
        compiler_params=pltpu.CompilerParams(dimension_semantics=("arbitrary",),
                                             vmem_limit_bytes=VMEM_LIMIT),
        name="combine",
    )(dest3, dest3, yg, gates_tok, x1, mod3, npf2)


def _pad_cols(w, cols):
    return jnp.pad(w, ((0, 0), (0, cols - w.shape[1])))


def kernel(x, c, ctx, c_ctx, w_ada, b_ada, norm_pre_mix, norm_post_mix, norm_pre_ffn, norm_post_ffn,
           w_in, gdn_conv_w, gdn_A_log, gdn_dt_bias, gdn_norm_w, ssd_conv_w, ssd_conv_b, ssd_A_log,
           ssd_dt_bias, ssd_D, ssd_norm_w, w_out, w_router, b_router, w_gate, b_gate, w_up, b_up,
           w_down, b_down):
    bn, t, d = x.shape
    n = bn * t
    l = 0
    mod_rows = -(-(bn + 1) // SUBLANE) * SUBLANE
    cc = jnp.zeros((mod_rows, d), F32).at[:bn].set(c).at[bn].set(c_ctx)
    mod3 = _ada(cc, w_ada[l], b_ada[l]).reshape(mod_rows, 1, 6 * d)

    w_in_b = w_in[l].astype(BF16)
    w_gdn = _pad_cols(w_in_b[:, :GDN_COLS], GDN_COLS_PAD)
    w_ssd = _pad_cols(w_in_b[:, GDN_COLS:], SSD_COLS_PAD)
    npm1 = norm_pre_mix[l].reshape(1, d)
    ug = _inproj(x, mod3, npm1, w_gdn, mod_row=None, col_major=False)
    us = _inproj(x, mod3, npm1, w_ssd, mod_row=None, col_major=True)
    ugc = _inproj(ctx, mod3, npm1, w_gdn, mod_row=bn, col_major=False)
    usc = _inproj(ctx, mod3, npm1, w_ssd, mod_row=bn, col_major=False)

    gdn_o = _gdn(ug, ugc, gdn_conv_w[l], _gdn_gate_params(gdn_A_log[l], gdn_dt_bias[l]),
                 jnp.tile(gdn_norm_w[l], 2).reshape(1, LANE))
    ssd_o = _ssd(us, usc, ssd_conv_w[l], ssd_conv_b[l].reshape(1, SSD_CONV_CH),
                 _ssd_gate_params(ssd_A_log[l], ssd_dt_bias[l]),
                 jnp.repeat(ssd_D[l], SSD_HEADDIM).reshape(SSD_GROUPS, 1, XW),
                 ssd_norm_w[l].reshape(SSD_GROUPS, 1, XW))

    x1, h2, e_t, gates_tok = _mix(
        gdn_o, ssd_o, x, mod3, norm_post_mix[l].reshape(1, d), norm_pre_ffn[l].reshape(1, d),
        w_out[l].astype(BF16), _pad_cols(w_router[l], LANE), _pad_cols(b_router[l].reshape(1, N_EXPERTS), LANE))

    n_blocks = -(-(n * TOP_K) // MOE_BLOCK) + N_EXPERTS
    cap = n_blocks * MOE_BLOCK
    ntile = n // TOK_TILE
    ids2 = e_t.reshape(SUBLANE * ntile, LANE)[:TOP_K * ntile]
    dest, meta = _rank(ids2, n_blocks)
    dest3 = dest.reshape(TOP_K, ntile, LANE).transpose(1, 0, 2)
    block_e = meta[:SUBLANE].reshape(-1)[:n_blocks]
    n_used = meta[SUBLANE, :1]

    xg = _dispatch(meta, dest3, h2, cap)
    yg = _experts(block_e, n_used, xg,
                  w_gate[l], w_up[l], w_down[l],
                  b_gate[l].reshape(N_EXPERTS, 1, D_EXPERT), b_up[l].reshape(N_EXPERTS, 1, D_EXPERT),
                  b_down[l].reshape(N_EXPERTS, 1, d))
    out = _combine(dest3, yg, gates_tok.reshape(n, LANE), x1.reshape(n, d), mod3,
                   norm_post_ffn[l].reshape(1, d), t)
    return out.reshape(bn, t, d)
```
